```python
import jax, jax.numpy as jnp
from jax import lax
import numpy as np

D_MODEL = 2048
BATCH = 1
SEQ = 16384
DEPTH = 1

D_MIX = D_MODEL
D_LRU = D_MIX // 2
N_LRU_BLOCKS = 16
LRU_BLOCK = D_LRU // N_LRU_BLOCKS
CONV_WIDTH = 4
LRU_C = 8.0
D_ATT = D_MIX - D_LRU
ATT_HEAD_DIM = 128
N_ATT_HEADS = D_ATT // ATT_HEAD_DIM
Q_BLOCK = 128
D_FF = 4 * D_MODEL
N_MOD = 6
EPS = 1e-6
D_IN_PROJ = 2 * D_LRU + 3 * D_ATT + N_ATT_HEADS

kernel_name = "hymba_rglru_fox_sandwich_adaln"


def rms_norm(x, g):
    xf = x.astype(jnp.float32)
    y = xf * lax.rsqrt(jnp.mean(jnp.square(xf), axis=-1, keepdims=True) + EPS)
    return (y * g.astype(jnp.float32)).astype(x.dtype)


def modulate(h, shift, scale):
    return h * (1 + scale[:, None, :]) + shift[:, None, :]


def causal_depthwise_conv(u, w, b):
    C = u.shape[-1]
    out = lax.conv_general_dilated(
        u, w[:, None, :], window_strides=(1,), padding=[(CONV_WIDTH - 1, 0)],
        dimension_numbers=("NWC", "WIO", "NWC"), feature_group_count=C)
    return out + b


def rg_lru(xc, w_rg, b_rg, w_ig, b_ig, lam):
    B, S, C = xc.shape
    xb = xc.reshape(B, S, N_LRU_BLOCKS, LRU_BLOCK)
    r = jax.nn.sigmoid(jnp.einsum("bsni,nij->bsnj", xb, w_rg).reshape(B, S, C) + b_rg)
    i = jax.nn.sigmoid(jnp.einsum("bsni,nij->bsnj", xb, w_ig).reshape(B, S, C) + b_ig)
    log_a = -LRU_C * r.astype(jnp.float32) * jax.nn.softplus(-lam.astype(jnp.float32))
    a = jnp.exp(log_a)
    mult = jnp.sqrt(-jnp.expm1(2.0 * log_a))
    bx = mult * (i * xc).astype(jnp.float32)

    def combine(left, right):
        a1, b1 = left
        a2, b2 = right
        return a1 * a2, a2 * b1 + b2

    _, h = lax.associative_scan(combine, (a, bx), axis=1)
    return h.astype(xc.dtype)


def fox_attention(q, k, v, log_f):
    B, S, H, Dh = q.shape
    n_blocks = S // Q_BLOCK
    F = jnp.cumsum(log_f, axis=1).transpose(0, 2, 1)
    qb = q.reshape(B, n_blocks, Q_BLOCK, H, Dh).transpose(1, 0, 2, 3, 4)
    Fq = F.reshape(B, H, n_blocks, Q_BLOCK).transpose(2, 0, 1, 3)
    key_pos = jnp.arange(S)
    scale = Dh ** -0.5

    def attend_block(args):
        qi, Fi, blk = args
        s = jnp.einsum("bqhd,bkhd->bhqk", qi, k, preferred_element_type=jnp.float32) * scale
        bias = Fi[..., :, None] - F[..., None, :]
        q_pos = blk * Q_BLOCK + jnp.arange(Q_BLOCK)
        causal = key_pos[None, :] <= q_pos[:, None]
        logits = jnp.where(causal, s + bias, -jnp.inf)
        p = jax.nn.softmax(logits, axis=-1)
        return jnp.einsum("bhqk,bkhd->bqhd", p.astype(v.dtype), v)

    out = lax.map(attend_block, (qb, Fq, jnp.arange(n_blocks)))
    return out.transpose(1, 0, 2, 3, 4).reshape(B, S, H, Dh)


def setup_inputs(seed: int = 0) -> dict:
    key = jax.random.key(seed)
    ks = jax.random.split(key, 24)
    L, D = DEPTH, D_MODEL
    nrm = jax.random.normal

    def gain(k, n):
        return 1.0 + 0.05 * nrm(k, (L, n), jnp.float32)

    a0 = jax.random.uniform(ks[12], (L, D_LRU), jnp.float32, minval=0.9, maxval=0.999)
    return {
        "x": nrm(ks[0], (BATCH, SEQ, D), jnp.float32),
        "c": nrm(ks[1], (BATCH, D), jnp.float32),
        "w_ada": 0.5 * nrm(ks[2], (L, D, N_MOD * D), jnp.float32) * D ** -0.5,
        "b_ada": 0.01 * nrm(ks[3], (L, N_MOD * D), jnp.float32),
        "g_pre_mix": gain(ks[4], D),
        "g_post_mix": gain(ks[5], D),
        "g_pre_mlp": gain(ks[6], D),
        "g_post_mlp": gain(ks[7], D),
        "w_in": nrm(ks[8], (L, D, D_IN_PROJ), jnp.float32) * D ** -0.5,
        "conv_w": nrm(ks[9], (L, CONV_WIDTH, D_LRU), jnp.float32) * CONV_WIDTH ** -0.5,
        "conv_b": 0.01 * nrm(ks[10], (L, D_LRU), jnp.float32),
        "w_rg": nrm(ks[11], (L, N_LRU_BLOCKS, LRU_BLOCK, LRU_BLOCK), jnp.float32) * LRU_BLOCK ** -0.5,
        "b_rg": 0.01 * nrm(ks[13], (L, D_LRU), jnp.float32),
        "w_ig": nrm(ks[14], (L, N_LRU_BLOCKS, LRU_BLOCK, LRU_BLOCK), jnp.float32) * LRU_BLOCK ** -0.5,
        "b_ig": 0.01 * nrm(ks[15], (L, D_LRU), jnp.float32),
        "lru_lambda": jnp.log(a0) - jnp.log1p(-a0),
        "b_forget": jax.random.uniform(ks[16], (L, N_ATT_HEADS), jnp.float32, minval=1.0, maxval=4.0),
        "g_lru_out": gain(ks[17], D_LRU),
        "g_att_out": gain(ks[18], D_ATT),
        "w_out": nrm(ks[19], (L, D_MIX, D), jnp.float32) * D_MIX ** -0.5,
        "w_ff1": nrm(ks[20], (L, D, D_FF), jnp.float32) * D ** -0.5,
        "w_ff2": nrm(ks[21], (L, D_FF, D), jnp.float32) * D_FF ** -0.5,
    }


def reference(x, c, w_ada, b_ada, g_pre_mix, g_post_mix, g_pre_mlp, g_post_mlp, w_in, conv_w, conv_b,
              w_rg, b_rg, w_ig, b_ig, lru_lambda, b_forget, g_lru_out, g_att_out, w_out, w_ff1, w_ff2):
    B, S, _ = x.shape
    c_act = jax.nn.silu(c)
    split_at = [D_LRU, 2 * D_LRU, 2 * D_LRU + D_ATT, 2 * D_LRU + 2 * D_ATT, 2 * D_LRU + 3 * D_ATT]
    for l in range(DEPTH):
        mod = c_act @ w_ada[l] + b_ada[l]
        sh1, sc1, gt1, sh2, sc2, gt2 = jnp.split(mod, N_MOD, axis=-1)

        h = modulate(rms_norm(x, g_pre_mix[l]), sh1, sc1)
        proj = h @ w_in[l]
        u_lru, u_gate, q, k, v, f_logit = jnp.split(proj, split_at, axis=-1)

        xc = causal_depthwise_conv(u_lru, conv_w[l], conv_b[l])
        y_lru = rg_lru(xc, w_rg[l], b_rg[l], w_ig[l], b_ig[l], lru_lambda[l]) * jax.nn.gelu(u_gate)

        q = q.reshape(B, S, N_ATT_HEADS, ATT_HEAD_DIM)
        k = k.reshape(B, S, N_ATT_HEADS, ATT_HEAD_DIM)
        v = v.reshape(B, S, N_ATT_HEADS, ATT_HEAD_DIM)
        log_f = jax.nn.log_sigmoid((f_logit + b_forget[l]).astype(jnp.float32))
        y_att = fox_attention(q, k, v, log_f).reshape(B, S, D_ATT)

        y = jnp.concatenate([rms_norm(y_lru, g_lru_out[l]), rms_norm(y_att, g_att_out[l])], axis=-1) @ w_out[l]
        x = x + gt1[:, None, :] * rms_norm(y, g_post_mix[l])

        h = modulate(rms_norm(x, g_pre_mlp[l]), sh2, sc2)
        y = jnp.square(jax.nn.relu(h @ w_ff1[l])) @ w_ff2[l]
        x = x + gt2[:, None, :] * rms_norm(y, g_post_mlp[l])
    return x
```

```python
import functools
import math

import jax
import jax.numpy as jnp
from jax import lax
from jax.experimental import pallas as pl
from jax.experimental.pallas import tpu as pltpu

F32 = jnp.float32
BF16 = jnp.bfloat16

EPS = 1e-6
LRU_C = 8.0
CONV_WIDTH = 4
HEAD_DIM = 128
LANES = 128
SUBLANES = 8
VMEM_LIMIT_BYTES = 56 * 1024 * 1024


def _params(*sem):
    return pltpu.CompilerParams(dimension_semantics=sem, vmem_limit_bytes=VMEM_LIMIT_BYTES)


def _sigmoid(z):
    return 1.0 / (1.0 + jnp.exp(-z))


def _rms_scale(v):
    return lax.rsqrt(jnp.mean(v * v, axis=-1, keepdims=True) + EPS)


def _resident(shape):
    nd = len(shape)
    return pl.BlockSpec(shape, lambda *_: (0,) * nd, pipeline_mode=pl.Buffered(1))


def _adaln_kernel(c_ref, w_ref, b_ref, o_ref):
    c = c_ref[...]
    ca = c * _sigmoid(c)
    o_ref[...] = jnp.dot(ca, w_ref[...], preferred_element_type=F32,
                         precision=lax.Precision.HIGHEST) + b_ref[...]


def _adaln(c8, w_ada, b_ada, tn=1024):
    d, n = w_ada.shape
    return pl.pallas_call(
        _adaln_kernel,
        grid=(n // tn,),
        in_specs=[pl.BlockSpec((SUBLANES, d), lambda j: (0, 0)),
                  pl.BlockSpec((d, tn), lambda j: (0, j)),
                  pl.BlockSpec((1, tn), lambda j: (0, j))],
        out_specs=pl.BlockSpec((SUBLANES, tn), lambda j: (0, j)),
        out_shape=jax.ShapeDtypeStruct((SUBLANES, n), F32),
        compiler_params=_params("arbitrary"),
        name="adaln",
    )(c8, w_ada, b_ada)


def _in_proj_kernel(x_ref, g_ref, sh_ref, sc_ref, wu_ref, wq_ref, wk_ref, wv_ref, wf_ref,
                    u_ref, q_ref, k_ref, v_ref, f_ref, *, n_heads, q_scale):
    x = x_ref[...]
    h = x * _rms_scale(x) * g_ref[...]
    h = (h * (1.0 + sc_ref[...]) + sh_ref[...]).astype(BF16)
    u_ref[...] = jnp.dot(h, wu_ref[...], preferred_element_type=F32).astype(BF16)
    q = jnp.dot(h, wq_ref[...], preferred_element_type=F32) * q_scale
    k = jnp.dot(h, wk_ref[...], preferred_element_type=F32)
    v = jnp.dot(h, wv_ref[...], preferred_element_type=F32)
    for hd in range(n_heads):
        sl = slice(hd * HEAD_DIM, (hd + 1) * HEAD_DIM)
        q_ref[hd] = q[:, sl].astype(BF16)
        k_ref[hd] = k[:, sl].astype(BF16)
        v_ref[hd] = v[:, sl].astype(BF16)
    f_ref[...] = jnp.dot(h, wf_ref[...], preferred_element_type=F32)


def _in_proj(x2d, g, sh, sc, wu, wq, wk, wv, wf, tm=512):
    s, d = x2d.shape
    n_u = wu.shape[1]
    n_heads = wq.shape[1] // HEAD_DIM
    row = lambda i: (i, 0)
    hrow = lambda i: (0, i, 0)
    vec = pl.BlockSpec((1, d), lambda i: (0, 0))
    kern = functools.partial(_in_proj_kernel, n_heads=n_heads, q_scale=HEAD_DIM ** -0.5)
    qkv_shape = jax.ShapeDtypeStruct((n_heads, s, HEAD_DIM), BF16)
    qkv_spec = pl.BlockSpec((n_heads, tm, HEAD_DIM), hrow)
    return pl.pallas_call(
        kern,
        grid=(s // tm,),
        in_specs=[pl.BlockSpec((tm, d), row), vec, vec, vec,
                  _resident(wu.shape), _resident(wq.shape), _resident(wk.shape),
                  _resident(wv.shape), _resident(wf.shape)],
        out_specs=[pl.BlockSpec((tm, n_u), row), qkv_spec, qkv_spec, qkv_spec,
                   pl.BlockSpec((tm, LANES), row)],
        out_shape=[jax.ShapeDtypeStruct((s, n_u), BF16), qkv_shape, qkv_shape, qkv_shape,
                   jax.ShapeDtypeStruct((s, LANES), F32)],
        compiler_params=_params("arbitrary"),
        name="in_proj",
    )(x2d, g, sh, sc, wu, wq, wk, wv, wf)


def _fcumsum_kernel(f_ref, b_ref, o_ref, carry_ref, *, tm):
    @pl.when(pl.program_id(0) == 0)
    def _():
        carry_ref[...] = jnp.zeros_like(carry_ref)

    z = f_ref[...] + b_ref[...]
    zt = jnp.transpose(z)[:SUBLANES, :]
    lf = jnp.minimum(zt, 0.0) - jnp.log(1.0 + jnp.exp(-jnp.abs(zt)))
    lane = lax.broadcasted_iota(jnp.int32, lf.shape, 1)
    acc = lf
    d = 1
    while d < tm:
        acc = acc + jnp.where(lane >= d, pltpu.roll(acc, d, axis=1), 0.0)
        d *= 2
    acc = acc + carry_ref[...]
    o_ref[...] = acc
    carry_ref[...] = jnp.broadcast_to(acc[:, tm - 1:tm], carry_ref.shape)


def _fcumsum(f, b_pad, tm=1024):
    s = f.shape[0]
    return pl.pallas_call(
        functools.partial(_fcumsum_kernel, tm=tm),
        grid=(s // tm,),
        in_specs=[pl.BlockSpec((tm, LANES), lambda i: (i, 0)),
                  pl.BlockSpec((1, LANES), lambda i: (0, 0))],
        out_specs=pl.BlockSpec((SUBLANES, tm), lambda i: (0, i)),
        out_shape=jax.ShapeDtypeStruct((SUBLANES, s), F32),
        scratch_shapes=[pltpu.VMEM((SUBLANES, tm), F32)],
        compiler_params=_params("arbitrary"),
        name="fcumsum",
    )(f, b_pad)


def _rglru_kernel(ul_ref, ug_ref, cw_ref, cb_ref, wg_ref, brg_ref, big_ref, lam_ref, go_ref,
                  y_ref, ext_ref, a_ref, b_ref, hc_ref, *, tt, c):
    pad = SUBLANES

    @pl.when(pl.program_id(0) == 0)
    def _():
        ext_ref[0:pad, :] = jnp.zeros((pad, c), F32)
        hc_ref[...] = jnp.zeros_like(hc_ref)

    u = ul_ref[...].astype(F32)
    ext_ref[pad:pad + tt, :] = u
    xc = u * cw_ref[CONV_WIDTH - 1:CONV_WIDTH, :] + cb_ref[...]
    for k in range(CONV_WIDTH - 1):
        back = CONV_WIDTH - 1 - k
        xc = xc + ext_ref[pad - back:pad - back + tt, :] * cw_ref[k:k + 1, :]
    ext_ref[0:pad, :] = u[tt - pad:tt, :]

    gates = jnp.dot(xc.astype(BF16), wg_ref[...], preferred_element_type=F32)
    r = _sigmoid(gates[:, :c] + brg_ref[...])
    ig = _sigmoid(gates[:, c:] + big_ref[...])
    lam = lam_ref[...]
    sp = jnp.maximum(-lam, 0.0) + jnp.log(1.0 + jnp.exp(-jnp.abs(lam)))
    log_a = (-LRU_C) * r * sp
    a = jnp.exp(log_a)
    mult = jnp.sqrt(1.0 - a * a)
    a_ref[...] = a
    b_ref[...] = mult * (ig * xc)

    row = lax.broadcasted_iota(jnp.int32, (SUBLANES, c), 0)

    def group(gi, h_prev):
        r0 = pl.multiple_of(gi * SUBLANES, SUBLANES)
        av = a_ref[pl.ds(r0, SUBLANES), :]
        bv = b_ref[pl.ds(r0, SUBLANES), :]
        d = 1
        while d < SUBLANES:
            a_sh = jnp.where(row >= d, pltpu.roll(av, d, axis=0), 1.0)
            b_sh = jnp.where(row >= d, pltpu.roll(bv, d, axis=0), 0.0)
            bv = bv + av * b_sh
            av = av * a_sh
            d *= 2
        hv = bv + av * h_prev
        b_ref[pl.ds(r0, SUBLANES), :] = hv
        return jnp.broadcast_to(hv[SUBLANES - 1:SUBLANES, :], (SUBLANES, c))

    hc_ref[...] = lax.fori_loop(0, tt // SUBLANES, group, hc_ref[...], unroll=4)

    g = ug_ref[...].astype(F32)
    gelu = 0.5 * g * (1.0 + jnp.tanh(math.sqrt(2.0 / math.pi) * (g + 0.044715 * (g * g * g))))
    y = b_ref[...] * gelu
    y_ref[...] = (y * _rms_scale(y) * go_ref[...]).astype(BF16)


def _rglru(u, conv_w, conv_b, w_gates, b_rg, b_ig, lam, g_out, tt=256):
    s = u.shape[0]
    c = conv_w.shape[1]
    vec = pl.BlockSpec((1, c), lambda i: (0, 0))
    return pl.pallas_call(
        functools.partial(_rglru_kernel, tt=tt, c=c),
        grid=(s // tt,),
        in_specs=[pl.BlockSpec((tt, c), lambda i: (i, 0)),
                  pl.BlockSpec((tt, c), lambda i: (i, 1)),
                  pl.BlockSpec((CONV_WIDTH, c), lambda i: (0, 0)),
                  vec, _resident(w_gates.shape), vec, vec, vec, vec],
        out_specs=pl.BlockSpec((tt, c), lambda i: (i, 0)),
        out_shape=jax.ShapeDtypeStruct((s, c), BF16),
        scratch_shapes=[pltpu.VMEM((tt + SUBLANES, c), F32),
                        pltpu.VMEM((tt, c), F32),
                        pltpu.VMEM((tt, c), F32),
                        pltpu.VMEM((SUBLANES, c), F32)],
        compiler_params=_params("arbitrary"),
        name="rglru",
    )(u, u, conv_w, conv_b, w_gates, b_rg, b_ig, lam, g_out)


def _fox_kernel(q_ref, k_ref, v_ref, fq_ref, fk_ref, o_ref, m_ref, l_ref, acc_ref, *, tq):
    qi = pl.program_id(1)
    q = q_ref[0]
    c0 = fq_ref[0][:, 0:1]
    m_ref[...] = jnp.full_like(m_ref, -jnp.inf)
    l_ref[...] = jnp.zeros_like(l_ref)
    acc_ref[...] = jnp.zeros_like(acc_ref)

    def step(ki, masked):
        k0 = pl.multiple_of(ki * tq, tq)
        kb = k_ref[0, pl.ds(k0, tq), :]
        vb = v_ref[0, pl.ds(k0, tq), :]
        s = lax.dot_general(q, kb, (((1,), (1,)), ((), ())), preferred_element_type=F32)
        s = s + (c0 - fk_ref[0, :, pl.ds(k0, tq)])
        if masked:
            rr = lax.broadcasted_iota(jnp.int32, (tq, tq), 0)
            cc = lax.broadcasted_iota(jnp.int32, (tq, tq), 1)
            s = jnp.where(cc <= rr, s, -jnp.inf)
        m_prev = m_ref[...]
        m_new = jnp.maximum(m_prev, jnp.max(s, axis=-1, keepdims=True))
        alpha = jnp.exp(m_prev - m_new)
        p = jnp.exp(s - m_new)
        l_ref[...] = alpha * l_ref[...] + jnp.sum(p, axis=-1, keepdims=True)
        acc_ref[...] = alpha * acc_ref[...] + jnp.dot(p.astype(BF16), vb, preferred_element_type=F32)
        m_ref[...] = m_new

    def body(ki, carry):
        step(ki, False)
        return carry

    lax.fori_loop(0, qi, body, 0)
    step(qi, True)
    o_ref[...] = (acc_ref[...] / l_ref[...]).astype(o_ref.dtype)


def _fox_attention(q, k, v, ft3, tq=512):
    n_heads, s, dh = q.shape
    kern = functools.partial(_fox_kernel, tq=tq)
    head = lambda h, i: (h, 0, 0)
    return pl.pallas_call(
        kern,
        grid=(n_heads, s // tq),
        in_specs=[pl.BlockSpec((1, tq, dh), lambda h, i: (h, i, 0)),
                  pl.BlockSpec((1, s, dh), head),
                  pl.BlockSpec((1, s, dh), head),
                  pl.BlockSpec((1, 1, tq), lambda h, i: (h, 0, i)),
                  pl.BlockSpec((1, 1, s), head)],
        out_specs=pl.BlockSpec((tq, dh), lambda h, i: (i, h)),
        out_shape=jax.ShapeDtypeStruct((s, n_heads * dh), BF16),
        scratch_shapes=[pltpu.VMEM((tq, 1), F32), pltpu.VMEM((tq, 1), F32),
                        pltpu.VMEM((tq, dh), F32)],
        compiler_params=_params("arbitrary", "arbitrary"),
        name="fox_attn",
    )(q, k, v, ft3, ft3)


def _out_proj_kernel(yl_ref, ya_ref, x_ref, ga_ref, wl_ref, wa_ref, gp_ref, gt_ref,
                     g2_ref, sh_ref, sc_ref, x1_ref, h2_ref):
    ya = ya_ref[...].astype(F32)
    ya = (ya * _rms_scale(ya) * ga_ref[...]).astype(BF16)
    y = jnp.dot(yl_ref[...], wl_ref[...], preferred_element_type=F32)
    y = y + jnp.dot(ya, wa_ref[...], preferred_element_type=F32)
    x1 = x_ref[...] + gt_ref[...] * (y * _rms_scale(y) * gp_ref[...])
    x1_ref[...] = x1
    h2 = x1 * _rms_scale(x1) * g2_ref[...]
    h2_ref[...] = (h2 * (1.0 + sc_ref[...]) + sh_ref[...]).astype(BF16)


def _out_proj(yl, ya, x2d, g_att, w_l, w_a, g_post, gt1, g_pre2, sh2, sc2, tm=512):
    s, d = x2d.shape
    c = yl.shape[1]
    row = lambda i: (i, 0)
    vec_d = pl.BlockSpec((1, d), lambda i: (0, 0))
    vec_c = pl.BlockSpec((1, c), lambda i: (0, 0))
    return pl.pallas_call(
        _out_proj_kernel,
        grid=(s // tm,),
        in_specs=[pl.BlockSpec((tm, c), row), pl.BlockSpec((tm, c), row), pl.BlockSpec((tm, d), row),
                  vec_c, _resident(w_l.shape), _resident(w_a.shape), vec_d, vec_d,
                  vec_d, vec_d, vec_d],
        out_specs=[pl.BlockSpec((tm, d), row), pl.BlockSpec((tm, d), row)],
        out_shape=[jax.ShapeDtypeStruct((s, d), F32), jax.ShapeDtypeStruct((s, d), BF16)],
        compiler_params=_params("arbitrary"),
        name="out_proj",
    )(yl, ya, x2d, g_att, w_l, w_a, g_post, gt1, g_pre2, sh2, sc2)


def _ffn_kernel(h_ref, w1_ref, w2_ref, x_ref, gp_ref, gt_ref, o_ref, acc_ref):
    j = pl.program_id(1)

    @pl.when(j == 0)
    def _():
        acc_ref[...] = jnp.zeros_like(acc_ref)

    hid = jnp.dot(h_ref[...], w1_ref[...], preferred_element_type=F32)
    hid = jnp.maximum(hid, 0.0)
    hid = (hid * hid).astype(BF16)
    acc_ref[...] += jnp.dot(hid, w2_ref[...], preferred_element_type=F32)

    @pl.when(j == pl.num_programs(1) - 1)
    def _():
        y = acc_ref[...]
        o_ref[...] = x_ref[...] + gt_ref[...] * (y * _rms_scale(y) * gp_ref[...])


def _ffn(h2, w1, w2, x1, g_post, gt2, tm=512, tf=1024):
    s, d = x1.shape
    dff = w1.shape[1]
    vec = pl.BlockSpec((1, d), lambda i, j: (0, 0))
    return pl.pallas_call(
        _ffn_kernel,
        grid=(s // tm, dff // tf),
        in_specs=[pl.BlockSpec((tm, d), lambda i, j: (i, 0)),
                  pl.BlockSpec((d, tf), lambda i, j: (0, j)),
                  pl.BlockSpec((tf, d), lambda i, j: (j, 0)),
                  pl.BlockSpec((tm, d), lambda i, j: (i, 0)),
                  vec, vec],
        out_specs=pl.BlockSpec((tm, d), lambda i, j: (i, 0)),
        out_shape=jax.ShapeDtypeStruct((s, d), F32),
        scratch_shapes=[pltpu.VMEM((tm, d), F32)],
        compiler_params=_params("parallel", "arbitrary"),
        name="ffn",
    )(h2, w1, w2, x1, g_post, gt2)


def _block_diag(w):
    n, bs, _ = w.shape
    eye = jnp.eye(n, dtype=w.dtype)
    return jnp.einsum("nij,nm->nimj", w, eye).reshape(n * bs, n * bs)


def kernel(x, c, w_ada, b_ada, g_pre_mix, g_post_mix, g_pre_mlp, g_post_mlp, w_in, conv_w, conv_b,
           w_rg, b_rg, w_ig, b_ig, lru_lambda, b_forget, g_lru_out, g_att_out, w_out, w_ff1, w_ff2):
    bsz, s, d = x.shape
    assert bsz == 1, "kernels are written for a single sequence"
    depth = w_ada.shape[0]
    d_lru = conv_w.shape[-1]
    n_heads = b_forget.shape[-1]
    d_att = n_heads * HEAD_DIM
    assert n_heads <= SUBLANES

    x2d = x.reshape(s, d)
    c8 = jnp.broadcast_to(c, (SUBLANES, d))
    for l in range(depth):
        mod = _adaln(c8, w_ada[l], b_ada[l][None, :])[0:1]
        sh1, sc1, gt1, sh2, sc2, gt2 = jnp.split(mod, 6, axis=-1)

        w = w_in[l].astype(BF16)
        o_q = 2 * d_lru
        wu, wq = w[:, :o_q], w[:, o_q:o_q + d_att]
        wk, wv = w[:, o_q + d_att:o_q + 2 * d_att], w[:, o_q + 2 * d_att:o_q + 3 * d_att]
        wf = jnp.pad(w[:, o_q + 3 * d_att:], ((0, 0), (0, LANES - n_heads)))
        u, q, k, v, f = _in_proj(x2d, g_pre_mix[l][None], sh1, sc1, wu, wq, wk, wv, wf)

        b_pad = jnp.pad(b_forget[l], (0, LANES - n_heads))[None, :]
        ft = _fcumsum(f, b_pad)
        ft3 = ft.reshape(SUBLANES, 1, s)

        w_gates = jnp.concatenate([_block_diag(w_rg[l]), _block_diag(w_ig[l])], axis=1).astype(BF16)
        y_lru = _rglru(u, conv_w[l], conv_b[l][None], w_gates, b_rg[l][None], b_ig[l][None],
                       lru_lambda[l][None], g_lru_out[l][None])
        y_att = _fox_attention(q, k, v, ft3)

        wo = w_out[l].astype(BF16)
        x2d, h2 = _out_proj(y_lru, y_att, x2d, g_att_out[l][None], wo[:d_lru], wo[d_lru:],
                            g_post_mix[l][None], gt1, g_pre_mlp[l][None], sh2, sc2)
        x2d = _ffn(h2, w_ff1[l].astype(BF16), w_ff2[l].astype(BF16), x2d, g_post_mlp[l][None], gt2)
    return x2d.reshape(bsz, s, d)
```

```python
import functools
import math

import jax
import jax.numpy as jnp
from jax import lax
from jax.experimental import pallas as pl
from jax.experimental.pallas import tpu as pltpu

F32 = jnp.float32
BF16 = jnp.bfloat16

EPS = 1e-6
LRU_C = 8.0
CONV_WIDTH = 4
HEAD_DIM = 128
LOG2E = math.log2(math.e)
LANES = 128
SUBLANES = 8
BF16_ROWS = 16
ATT_TK = 512
ATT_TQ = 1024
N_BIAS_PIECES = 3
VMEM_LIMIT_BYTES = 56 * 1024 * 1024


def _params(*sem):
    return pltpu.CompilerParams(dimension_semantics=sem, vmem_limit_bytes=VMEM_LIMIT_BYTES)


def _sigmoid(z):
    return 1.0 / (1.0 + jnp.exp(-z))


def _rms_scale(v):
    return lax.rsqrt(jnp.mean(v * v, axis=-1, keepdims=True) + EPS)


def _resident(shape):
    nd = len(shape)
    return pl.BlockSpec(shape, lambda *_: (0,) * nd, pipeline_mode=pl.Buffered(1))


def _adaln_kernel(c_ref, w_ref, b_ref, o_ref):
    c = c_ref[...]
    ca = c * _sigmoid(c)
    o_ref[...] = jnp.dot(ca, w_ref[...], preferred_element_type=F32,
                         precision=lax.Precision.HIGHEST) + b_ref[...]


def _adaln(c8, w_ada, b_ada, tn=1024):
    d, n = w_ada.shape
    return pl.pallas_call(
        _adaln_kernel,
        grid=(n // tn,),
        in_specs=[pl.BlockSpec((SUBLANES, d), lambda j: (0, 0)),
                  pl.BlockSpec((d, tn), lambda j: (0, j)),
                  pl.BlockSpec((1, tn), lambda j: (0, j))],
        out_specs=pl.BlockSpec((SUBLANES, tn), lambda j: (0, j)),
        out_shape=jax.ShapeDtypeStruct((SUBLANES, n), F32),
        compiler_params=_params("arbitrary"),
        name="adaln",
    )(c8, w_ada, b_ada)


def _in_proj_kernel(x_ref, g_ref, sh_ref, sc_ref, wu_ref, wq_ref, wk_ref, wv_ref, wf_ref,
                    u_ref, qt_ref, k_ref, vt_ref, f_ref, *, n_heads, q_scale):
    x = x_ref[...]
    h = x * _rms_scale(x) * g_ref[...]
    h = (h * (1.0 + sc_ref[...]) + sh_ref[...]).astype(BF16)
    u_ref[...] = jnp.dot(h, wu_ref[...], preferred_element_type=F32).astype(BF16)
    q = jnp.dot(h, wq_ref[...], preferred_element_type=F32) * q_scale
    k = jnp.dot(h, wk_ref[...], preferred_element_type=F32)
    v = jnp.dot(h, wv_ref[...], preferred_element_type=F32)
    for hd in range(n_heads):
        sl = slice(hd * HEAD_DIM, (hd + 1) * HEAD_DIM)
        qt_ref[hd] = jnp.transpose(q[:, sl]).astype(BF16)
        k_ref[hd] = k[:, sl].astype(BF16)
        vt_ref[hd] = jnp.transpose(v[:, sl]).astype(BF16)
    f_ref[...] = jnp.dot(h, wf_ref[...], preferred_element_type=F32)


def _in_proj(x2d, g, sh, sc, wu, wq, wk, wv, wf, tm=512):
    s, d = x2d.shape
    n_u = wu.shape[1]
    n_heads = wq.shape[1] // HEAD_DIM
    row = lambda i: (i, 0)
    vec = pl.BlockSpec((1, d), lambda i: (0, 0))
    kern = functools.partial(_in_proj_kernel, n_heads=n_heads, q_scale=HEAD_DIM ** -0.5 * LOG2E)
    k_shape = jax.ShapeDtypeStruct((n_heads, s, HEAD_DIM), BF16)
    k_spec = pl.BlockSpec((n_heads, tm, HEAD_DIM), lambda i: (0, i, 0))
    t_shape = jax.ShapeDtypeStruct((n_heads, HEAD_DIM, s), BF16)
    t_spec = pl.BlockSpec((n_heads, HEAD_DIM, tm), lambda i: (0, 0, i))
    return pl.pallas_call(
        kern,
        grid=(s // tm,),
        in_specs=[pl.BlockSpec((tm, d), row), vec, vec, vec,
                  _resident(wu.shape), _resident(wq.shape), _resident(wk.shape),
                  _resident(wv.shape), _resident(wf.shape)],
        out_specs=[pl.BlockSpec((tm, n_u), row), t_spec, k_spec, t_spec,
                   pl.BlockSpec((tm, LANES), row)],
        out_shape=[jax.ShapeDtypeStruct((s, n_u), BF16), t_shape, k_shape, t_shape,
                   jax.ShapeDtypeStruct((s, LANES), F32)],
        compiler_params=_params("arbitrary"),
        name="in_proj",
    )(x2d, g, sh, sc, wu, wq, wk, wv, wf)


def _fgate_kernel(f_ref, b_ref, gp_ref, gabs_ref, carry_ref, *, tm, tk, n_heads):
    @pl.when(pl.program_id(0) == 0)
    def _():
        carry_ref[...] = jnp.zeros_like(carry_ref)

    z = f_ref[...] + b_ref[...]
    inc = (jnp.log(1.0 + jnp.exp(-jnp.abs(z))) - jnp.minimum(z, 0.0)) * LOG2E
    row = lax.broadcasted_iota(jnp.int32, (tm, LANES), 0)
    acc = inc
    d = 1
    while d < tm:
        acc = acc + jnp.where(row >= d, pltpu.roll(acc, d, axis=0), 0.0)
        d *= 2
    carry = carry_ref[0:1, :]
    gabs_ref[...] = acc + carry
    carry_ref[...] = jnp.broadcast_to(acc[tm - 1:tm, :] + carry, carry_ref.shape)

    lane = lax.broadcasted_iota(jnp.int32, (tk, LANES), 1)
    for blk in range(tm // tk):
        a = acc[blk * tk:(blk + 1) * tk, :]
        rel = a - a[0:1, :]
        for hd in range(n_heads):
            rem = jnp.broadcast_to(rel[:, hd:hd + 1], (tk, LANES))
            out = jnp.zeros((tk, LANES), F32)
            for piece in range(N_BIAS_PIECES):
                pc = rem.astype(BF16).astype(F32)
                out = jnp.where(lane == piece, pc, out)
                rem = rem - pc
            gp_ref[hd, blk * tk:(blk + 1) * tk, :] = out.astype(BF16)


def _fgate(f, b_pad, n_heads, tk, tm=1024):
    s = f.shape[0]
    return pl.pallas_call(
        functools.partial(_fgate_kernel, tm=tm, tk=tk, n_heads=n_heads),
        grid=(s // tm,),
        in_specs=[pl.BlockSpec((tm, LANES), lambda i: (i, 0)),
                  pl.BlockSpec((1, LANES), lambda i: (0, 0))],
        out_specs=[pl.BlockSpec((n_heads, tm, LANES), lambda i: (0, i, 0)),
                   pl.BlockSpec((tm, LANES), lambda i: (i, 0))],
        out_shape=[jax.ShapeDtypeStruct((n_heads, s, LANES), BF16),
                   jax.ShapeDtypeStruct((s, LANES), F32)],
        scratch_shapes=[pltpu.VMEM((SUBLANES, LANES), F32)],
        compiler_params=_params("arbitrary"),
        name="fgate",
    )(f, b_pad)


def _rglru_kernel(ul_ref, ug_ref, cw_ref, cb_ref, wg_ref, brg_ref, big_ref, lam_ref, go_ref,
                  y_ref, ext_ref, a_ref, b_ref, hc_ref, *, tt, c):
    pad = SUBLANES

    @pl.when(pl.program_id(0) == 0)
    def _():
        ext_ref[0:pad, :] = jnp.zeros((pad, c), F32)
        hc_ref[...] = jnp.zeros_like(hc_ref)

    u = ul_ref[...].astype(F32)
    ext_ref[pad:pad + tt, :] = u
    xc = u * cw_ref[CONV_WIDTH - 1:CONV_WIDTH, :] + cb_ref[...]
    for k in range(CONV_WIDTH - 1):
        back = CONV_WIDTH - 1 - k
        xc = xc + ext_ref[pad - back:pad - back + tt, :] * cw_ref[k:k + 1, :]
    ext_ref[0:pad, :] = u[tt - pad:tt, :]

    gates = jnp.dot(xc.astype(BF16), wg_ref[...], preferred_element_type=F32)
    r = 0.5 + 0.5 * jnp.tanh(0.5 * (gates[:, :c] + brg_ref[...]))
    ig = 0.5 + 0.5 * jnp.tanh(0.5 * (gates[:, c:] + big_ref[...]))
    lam = lam_ref[...]
    sp = jnp.maximum(-lam, 0.0) + jnp.log(1.0 + jnp.exp(-jnp.abs(lam)))
    log_a = (-LRU_C) * r * sp
    a = jnp.exp(log_a)
    t = 1.0 - a * a
    mult = jnp.where(t > 0.0, t * lax.rsqrt(t), 0.0)
    a_ref[...] = a
    b_ref[...] = mult * (ig * xc)

    row = lax.broadcasted_iota(jnp.int32, (SUBLANES, c), 0)

    def group(gi, h_prev):
        r0 = pl.multiple_of(gi * SUBLANES, SUBLANES)
        av = a_ref[pl.ds(r0, SUBLANES), :]
        bv = b_ref[pl.ds(r0, SUBLANES), :]
        d = 1
        while d < SUBLANES:
            a_sh = jnp.where(row >= d, pltpu.roll(av, d, axis=0), 1.0)
            b_sh = jnp.where(row >= d, pltpu.roll(bv, d, axis=0), 0.0)
            bv = bv + av * b_sh
            av = av * a_sh
            d *= 2
        hv = bv + av * h_prev
        b_ref[pl.ds(r0, SUBLANES), :] = hv
        return jnp.broadcast_to(hv[SUBLANES - 1:SUBLANES, :], (SUBLANES, c))

    hc_ref[...] = lax.fori_loop(0, tt // SUBLANES, group, hc_ref[...], unroll=4)

    g = ug_ref[...].astype(F32)
    gelu = 0.5 * g * (1.0 + jnp.tanh(math.sqrt(2.0 / math.pi) * (g + 0.044715 * (g * g * g))))
    y = b_ref[...] * gelu
    y_ref[...] = (y * _rms_scale(y) * go_ref[...]).astype(BF16)


def _rglru(u, conv_w, conv_b, w_gates, b_rg, b_ig, lam, g_out, tt=256):
    s = u.shape[0]
    c = conv_w.shape[1]
    vec = pl.BlockSpec((1, c), lambda i: (0, 0))
    return pl.pallas_call(
        functools.partial(_rglru_kernel, tt=tt, c=c),
        grid=(s // tt,),
        in_specs=[pl.BlockSpec((tt, c), lambda i: (i, 0)),
                  pl.BlockSpec((tt, c), lambda i: (i, 1)),
                  pl.BlockSpec((CONV_WIDTH, c), lambda i: (0, 0)),
                  vec, _resident(w_gates.shape), vec, vec, vec, vec],
        out_specs=pl.BlockSpec((tt, c), lambda i: (i, 0)),
        out_shape=jax.ShapeDtypeStruct((s, c), BF16),
        scratch_shapes=[pltpu.VMEM((tt + SUBLANES, c), F32),
                        pltpu.VMEM((tt, c), F32),
                        pltpu.VMEM((tt, c), F32),
                        pltpu.VMEM((SUBLANES, c), F32)],
        compiler_params=_params("arbitrary"),
        name="rglru",
    )(u, u, conv_w, conv_b, w_gates, b_rg, b_ig, lam, g_out)


def _fox_kernel(r_ref, qt_ref, k_ref, g_ref, vt_ref, o_ref, m_ref, acc_ref, s0_ref, s1_ref, cm_ref, *, tq, tk):
    assert tq == 2 * tk
    hd = pl.program_id(0)
    qi = pl.program_id(1)
    dh = qt_ref.shape[1]
    n_full = 2 * qi
    row = lax.broadcasted_iota(jnp.int32, (LANES, tq), 0)
    ones_q = jnp.where(row < N_BIAS_PIECES, 1.0, 0.0).astype(BF16)
    qt_aug = jnp.concatenate([qt_ref[0], ones_q], axis=0)
    ones_v = jnp.ones((BF16_ROWS, tk), BF16)
    r_q = r_ref[hd, n_full]
    m_ref[...] = jnp.full_like(m_ref, -jnp.inf)
    acc_ref[...] = jnp.zeros_like(acc_ref)
    s_refs = (s0_ref, s1_ref)

    def scores(kb, diag_offset, slot):
        k0 = pl.multiple_of(kb * tk, tk)
        k_aug = jnp.concatenate([k_ref[0, pl.ds(k0, tk), :], g_ref[0, pl.ds(k0, tk), :]], axis=1)
        s = jnp.dot(k_aug, qt_aug, preferred_element_type=F32)
        if diag_offset is not None:
            kk = lax.broadcasted_iota(jnp.int32, (tk, tq), 0) + diag_offset
            qq = lax.broadcasted_iota(jnp.int32, (tk, tq), 1)
            s = jnp.where(kk <= qq, s, -jnp.inf)
        s_refs[slot][...] = s
        cm_ref[slot] = jnp.max(s, axis=0, keepdims=True)

    def update(kb, slot):
        k0 = pl.multiple_of(kb * tk, tk)
        shift = r_ref[hd, kb] - r_q
        m_prev = m_ref[...]
        m_new = jnp.maximum(m_prev, cm_ref[slot] + shift)
        alpha = jnp.exp2(m_prev - m_new)
        p = jnp.exp2(s_refs[slot][...] - (m_new - shift)).astype(BF16)
        vt_aug = jnp.concatenate([vt_ref[0, :, pl.ds(k0, tk)], ones_v], axis=0)
        acc_ref[...] = alpha * acc_ref[...] + jnp.dot(vt_aug, p, preferred_element_type=F32)
        m_ref[...] = m_new

    scores(n_full, 0, 0)
    scores(n_full + 1, tk, 1)
    update(n_full, 0)

    def body(u, carry):
        scores(2 * u, None, 0)
        update(jnp.where(u == 0, n_full + 1, 2 * u - 1), 1)
        scores(2 * u + 1, None, 1)
        update(2 * u, 0)
        return carry

    lax.fori_loop(0, qi, body, 0)
    update(jnp.where(qi == 0, 1, n_full - 1), 1)
    out_t = acc_ref[0:dh, :] / acc_ref[dh:dh + 1, :]
    o_ref[...] = jnp.transpose(out_t).astype(o_ref.dtype)


def _fox_attention(r, qt, k, gp, vt, tq, tk):
    n_heads, s, dh = k.shape
    kern = functools.partial(_fox_kernel, tq=tq, tk=tk)
    head = lambda h, i: (h, 0, 0)
    return pl.pallas_call(
        kern,
        grid=(n_heads, s // tq),
        in_specs=[pl.BlockSpec(memory_space=pltpu.SMEM),
                  pl.BlockSpec((1, dh, tq), lambda h, i: (h, 0, i)),
                  pl.BlockSpec((1, s, dh), head),
                  pl.BlockSpec((1, s, LANES), head),
                  pl.BlockSpec((1, dh, s), head)],
        out_specs=pl.BlockSpec((tq, dh), lambda h, i: (i, h)),
        out_shape=jax.ShapeDtypeStruct((s, n_heads * dh), BF16),
        scratch_shapes=[pltpu.VMEM((1, tq), F32),
                        pltpu.VMEM((dh + BF16_ROWS, tq), F32),
                        pltpu.VMEM((tk, tq), F32), pltpu.VMEM((tk, tq), F32),
                        pltpu.VMEM((2, 1, tq), F32)],
        compiler_params=_params("arbitrary", "arbitrary"),
        name="fox_attn",
    )(r, qt, k, gp, vt)


def _out_proj_kernel(yl_ref, ya_ref, x_ref, ga_ref, wl_ref, wa_ref, gp_ref, gt_ref,
                     g2_ref, sh_ref, sc_ref, x1_ref, h2_ref):
    ya = ya_ref[...].astype(F32)
    ya = (ya * _rms_scale(ya) * ga_ref[...]).astype(BF16)
    y = jnp.dot(yl_ref[...], wl_ref[...], preferred_element_type=F32)
    y = y + jnp.dot(ya, wa_ref[...], preferred_element_type=F32)
    x1 = x_ref[...] + gt_ref[...] * (y * _rms_scale(y) * gp_ref[...])
    x1_ref[...] = x1
    h2 = x1 * _rms_scale(x1) * g2_ref[...]
    h2_ref[...] = (h2 * (1.0 + sc_ref[...]) + sh_ref[...]).astype(BF16)


def _out_proj(yl, ya, x2d, g_att, w_l, w_a, g_post, gt1, g_pre2, sh2, sc2, tm=512):
    s, d = x2d.shape
    c = yl.shape[1]
    row = lambda i: (i, 0)
    vec_d = pl.BlockSpec((1, d), lambda i: (0, 0))
    vec_c = pl.BlockSpec((1, c), lambda i: (0, 0))
    return pl.pallas_call(
        _out_proj_kernel,
        grid=(s // tm,),
        in_specs=[pl.BlockSpec((tm, c), row), pl.BlockSpec((tm, c), row), pl.BlockSpec((tm, d), row),
                  vec_c, _resident(w_l.shape), _resident(w_a.shape), vec_d, vec_d,
                  vec_d, vec_d, vec_d],
        out_specs=[pl.BlockSpec((tm, d), row), pl.BlockSpec((tm, d), row)],
        out_shape=[jax.ShapeDtypeStruct((s, d), F32), jax.ShapeDtypeStruct((s, d), BF16)],
        compiler_params=_params("arbitrary"),
        name="out_proj",
    )(yl, ya, x2d, g_att, w_l, w_a, g_post, gt1, g_pre2, sh2, sc2)


def _ffn_kernel(h_ref, w1_ref, w2_ref, x_ref, gp_ref, gt_ref, o_ref, acc_ref):
    j = pl.program_id(1)

    @pl.when(j == 0)
    def _():
        acc_ref[...] = jnp.zeros_like(acc_ref)

    hid = jnp.dot(h_ref[...], w1_ref[...], preferred_element_type=F32)
    hid = jnp.maximum(hid, 0.0)
    hid = (hid * hid).astype(BF16)
    acc_ref[...] += jnp.dot(hid, w2_ref[...], preferred_element_type=F32)

    @pl.when(j == pl.num_programs(1) - 1)
    def _():
        y = acc_ref[...]
        o_ref[...] = x_ref[...] + gt_ref[...] * (y * _rms_scale(y) * gp_ref[...])


def _ffn(h2, w1, w2, x1, g_post, gt2, tm=512, tf=1024):
    s, d = x1.shape
    dff = w1.shape[1]
    vec = pl.BlockSpec((1, d), lambda i, j: (0, 0))
    return pl.pallas_call(
        _ffn_kernel,
        grid=(s // tm, dff // tf),
        in_specs=[pl.BlockSpec((tm, d), lambda i, j: (i, 0)),
                  pl.BlockSpec((d, tf), lambda i, j: (0, j)),
                  pl.BlockSpec((tf, d), lambda i, j: (j, 0)),
                  pl.BlockSpec((tm, d), lambda i, j: (i, 0)),
                  vec, vec],
        out_specs=pl.BlockSpec((tm, d), lambda i, j: (i, 0)),
        out_shape=jax.ShapeDtypeStruct((s, d), F32),
        scratch_shapes=[pltpu.VMEM((tm, d), F32)],
        compiler_params=_params("parallel", "arbitrary"),
        name="ffn",
    )(h2, w1, w2, x1, g_post, gt2)


def _block_diag(w):
    n, bs, _ = w.shape
    eye = jnp.eye(n, dtype=w.dtype)
    return jnp.einsum("nij,nm->nimj", w, eye).reshape(n * bs, n * bs)


def kernel(x, c, w_ada, b_ada, g_pre_mix, g_post_mix, g_pre_mlp, g_post_mlp, w_in, conv_w, conv_b,
           w_rg, b_rg, w_ig, b_ig, lru_lambda, b_forget, g_lru_out, g_att_out, w_out, w_ff1, w_ff2):
    bsz, s, d = x.shape
    assert bsz == 1, "kernels are written for a single sequence"
    depth = w_ada.shape[0]
    d_lru = conv_w.shape[-1]
    n_heads = b_forget.shape[-1]
    d_att = n_heads * HEAD_DIM
    assert n_heads <= LANES

    x2d = x.reshape(s, d)
    c8 = jnp.broadcast_to(c, (SUBLANES, d))
    for l in range(depth):
        mod = _adaln(c8, w_ada[l], b_ada[l][None, :])[0:1]
        sh1, sc1, gt1, sh2, sc2, gt2 = jnp.split(mod, 6, axis=-1)

        w = w_in[l].astype(BF16)
        o_q = 2 * d_lru
        wu, wq = w[:, :o_q], w[:, o_q:o_q + d_att]
        wk, wv = w[:, o_q + d_att:o_q + 2 * d_att], w[:, o_q + 2 * d_att:o_q + 3 * d_att]
        wf = jnp.pad(w[:, o_q + 3 * d_att:], ((0, 0), (0, LANES - n_heads)))
        u, qt, k, vt, f = _in_proj(x2d, g_pre_mix[l][None], sh1, sc1, wu, wq, wk, wv, wf)

        b_pad = jnp.pad(b_forget[l], (0, LANES - n_heads))[None, :]
        gp, gabs = _fgate(f, b_pad, n_heads, ATT_TK)
        r = jnp.transpose(gabs[::ATT_TK, :n_heads])

        w_gates = jnp.concatenate([_block_diag(w_rg[l]), _block_diag(w_ig[l])], axis=1).astype(BF16)
        y_lru = _rglru(u, conv_w[l], conv_b[l][None], w_gates, b_rg[l][None], b_ig[l][None],
                       lru_lambda[l][None], g_lru_out[l][None])
        y_att = _fox_attention(r, qt, k, gp, vt, ATT_TQ, ATT_TK)

        wo = w_out[l].astype(BF16)
        x2d, h2 = _out_proj(y_lru, y_att, x2d, g_att_out[l][None], wo[:d_lru], wo[d_lru:],
                            g_post_mix[l][None], gt1, g_pre_mlp[l][None], sh2, sc2)
        x2d = _ffn(h2, w_ff1[l].astype(BF16), w_ff2[l].astype(BF16), x2d, g_post_mlp[l][None], gt2)
    return x2d.reshape(bsz, s, d)
```

```python
import functools
import math

import jax
import jax.numpy as jnp
from jax import lax
from jax.experimental import pallas as pl
from jax.experimental.pallas import tpu as pltpu

F32 = jnp.float32
BF16 = jnp.bfloat16

EPS = 1e-6
LRU_C = 8.0
CONV_WIDTH = 4
HEAD_DIM = 128
LOG2E = math.log2(math.e)
LANES = 128
SUBLANES = 8
BF16_ROWS = 16
ATT_TK = 512
ATT_TQ = 2048
N_BIAS_PIECES = 3
VMEM_LIMIT_BYTES = 56 * 1024 * 1024


def _params(*sem):
    return pltpu.CompilerParams(dimension_semantics=sem, vmem_limit_bytes=VMEM_LIMIT_BYTES)


def _sigmoid(z):
    return 1.0 / (1.0 + jnp.exp(-z))


def _rms_scale(v):
    return lax.rsqrt(jnp.mean(v * v, axis=-1, keepdims=True) + EPS)


def _resident(shape):
    nd = len(shape)
    return pl.BlockSpec(shape, lambda *_: (0,) * nd, pipeline_mode=pl.Buffered(1))


def _adaln_kernel(c_ref, w_ref, b_ref, o_ref):
    c = c_ref[...]
    ca = c * _sigmoid(c)
    o_ref[...] = jnp.dot(ca, w_ref[...], preferred_element_type=F32,
                         precision=lax.Precision.HIGHEST) + b_ref[...]


def _adaln(c8, w_ada, b_ada, tn=1024):
    d, n = w_ada.shape
    return pl.pallas_call(
        _adaln_kernel,
        grid=(n // tn,),
        in_specs=[pl.BlockSpec((SUBLANES, d), lambda j: (0, 0)),
                  pl.BlockSpec((d, tn), lambda j: (0, j)),
                  pl.BlockSpec((1, tn), lambda j: (0, j))],
        out_specs=pl.BlockSpec((SUBLANES, tn), lambda j: (0, j)),
        out_shape=jax.ShapeDtypeStruct((SUBLANES, n), F32),
        compiler_params=_params("arbitrary"),
        name="adaln",
    )(c8, w_ada, b_ada)


def _in_proj_kernel(x_ref, g_ref, sh_ref, sc_ref, wu_ref, wq_ref, wk_ref, wv_ref, wf_ref,
                    u_ref, qt_ref, k_ref, vt_ref, f_ref, *, n_heads, q_scale):
    x = x_ref[...]
    h = x * _rms_scale(x) * g_ref[...]
    h = (h * (1.0 + sc_ref[...]) + sh_ref[...]).astype(BF16)
    u_ref[...] = jnp.dot(h, wu_ref[...], preferred_element_type=F32).astype(BF16)
    q = jnp.dot(h, wq_ref[...], preferred_element_type=F32) * q_scale
    k = jnp.dot(h, wk_ref[...], preferred_element_type=F32)
    v = jnp.dot(h, wv_ref[...], preferred_element_type=F32)
    for hd in range(n_heads):
        sl = slice(hd * HEAD_DIM, (hd + 1) * HEAD_DIM)
        qt_ref[hd] = jnp.transpose(q[:, sl]).astype(BF16)
        k_ref[hd] = k[:, sl].astype(BF16)
        vt_ref[hd] = jnp.transpose(v[:, sl]).astype(BF16)
    f_ref[...] = jnp.dot(h, wf_ref[...], preferred_element_type=F32)


def _in_proj(x2d, g, sh, sc, w, wf, n_u, d_att, tm=512):
    s, d = x2d.shape
    n_heads = d_att // HEAD_DIM
    assert n_u % d_att == 0
    q_blk = n_u // d_att

    def cols(width, blk):
        return pl.BlockSpec((d, width), lambda i: (0, blk), pipeline_mode=pl.Buffered(1))

    row = lambda i: (i, 0)
    vec = pl.BlockSpec((1, d), lambda i: (0, 0))
    kern = functools.partial(_in_proj_kernel, n_heads=n_heads, q_scale=HEAD_DIM ** -0.5 * LOG2E)
    k_shape = jax.ShapeDtypeStruct((n_heads, s, HEAD_DIM), BF16)
    k_spec = pl.BlockSpec((n_heads, tm, HEAD_DIM), lambda i: (0, i, 0))
    t_shape = jax.ShapeDtypeStruct((n_heads, HEAD_DIM, s), BF16)
    t_spec = pl.BlockSpec((n_heads, HEAD_DIM, tm), lambda i: (0, 0, i))
    return pl.pallas_call(
        kern,
        grid=(s // tm,),
        in_specs=[pl.BlockSpec((tm, d), row), vec, vec, vec,
                  cols(n_u, 0), cols(d_att, q_blk), cols(d_att, q_blk + 1), cols(d_att, q_blk + 2),
                  _resident(wf.shape)],
        out_specs=[pl.BlockSpec((tm, n_u), row), t_spec, k_spec, t_spec,
                   pl.BlockSpec((tm, LANES), row)],
        out_shape=[jax.ShapeDtypeStruct((s, n_u), BF16), t_shape, k_shape, t_shape,
                   jax.ShapeDtypeStruct((s, LANES), F32)],
        compiler_params=_params("arbitrary"),
        name="in_proj",
    )(x2d, g, sh, sc, w, w, w, w, wf)


def _fgate_kernel(f_ref, b_ref, gp_ref, gabs_ref, carry_ref, *, tm, tk, n_heads):
    @pl.when(pl.program_id(0) == 0)
    def _():
        carry_ref[...] = jnp.zeros_like(carry_ref)

    z = f_ref[...] + b_ref[...]
    inc = (jnp.log(1.0 + jnp.exp(-jnp.abs(z))) - jnp.minimum(z, 0.0)) * LOG2E
    row = lax.broadcasted_iota(jnp.int32, (tm, LANES), 0)
    acc = inc
    d = 1
    while d < tm:
        acc = acc + jnp.where(row >= d, pltpu.roll(acc, d, axis=0), 0.0)
        d *= 2
    carry = carry_ref[0:1, :]
    gabs_ref[...] = acc + carry
    carry_ref[...] = jnp.broadcast_to(acc[tm - 1:tm, :] + carry, carry_ref.shape)

    lane = lax.broadcasted_iota(jnp.int32, (tk, LANES), 1)
    for blk in range(tm // tk):
        a = acc[blk * tk:(blk + 1) * tk, :]
        rel = a - a[0:1, :]
        for hd in range(n_heads):
            rem = jnp.broadcast_to(rel[:, hd:hd + 1], (tk, LANES))
            out = jnp.zeros((tk, LANES), F32)
            for piece in range(N_BIAS_PIECES):
                pc = rem.astype(BF16).astype(F32)
                out = jnp.where(lane == piece, pc, out)
                rem = rem - pc
            gp_ref[hd, blk * tk:(blk + 1) * tk, :] = out.astype(BF16)


def _fgate(f, b_pad, n_heads, tk, tm=1024):
    s = f.shape[0]
    return pl.pallas_call(
        functools.partial(_fgate_kernel, tm=tm, tk=tk, n_heads=n_heads),
        grid=(s // tm,),
        in_specs=[pl.BlockSpec((tm, LANES), lambda i: (i, 0)),
                  pl.BlockSpec((1, LANES), lambda i: (0, 0))],
        out_specs=[pl.BlockSpec((n_heads, tm, LANES), lambda i: (0, i, 0)),
                   pl.BlockSpec((tm, LANES), lambda i: (i, 0))],
        out_shape=[jax.ShapeDtypeStruct((n_heads, s, LANES), BF16),
                   jax.ShapeDtypeStruct((s, LANES), F32)],
        scratch_shapes=[pltpu.VMEM((SUBLANES, LANES), F32)],
        compiler_params=_params("arbitrary"),
        name="fgate",
    )(f, b_pad)


def _rglru_kernel(ul_ref, ug_ref, cw_ref, cb_ref, wg_ref, brg_ref, big_ref, lam_ref, go_ref,
                  y_ref, ext_ref, a_ref, b_ref, hc_ref, *, tt, c):
    pad = SUBLANES

    @pl.when(pl.program_id(0) == 0)
    def _():
        ext_ref[0:pad, :] = jnp.zeros((pad, c), F32)
        hc_ref[...] = jnp.zeros_like(hc_ref)

    u = ul_ref[...].astype(F32)
    ext_ref[pad:pad + tt, :] = u
    xc = u * cw_ref[CONV_WIDTH - 1:CONV_WIDTH, :] + cb_ref[...]
    for k in range(CONV_WIDTH - 1):
        back = CONV_WIDTH - 1 - k
        xc = xc + ext_ref[pad - back:pad - back + tt, :] * cw_ref[k:k + 1, :]
    ext_ref[0:pad, :] = u[tt - pad:tt, :]

    gates = jnp.dot(xc.astype(BF16), wg_ref[...], preferred_element_type=F32)
    r = 0.5 + 0.5 * jnp.tanh(0.5 * (gates[:, :c] + brg_ref[...]))
    ig = 0.5 + 0.5 * jnp.tanh(0.5 * (gates[:, c:] + big_ref[...]))
    lam = lam_ref[...]
    sp = jnp.maximum(-lam, 0.0) + jnp.log(1.0 + jnp.exp(-jnp.abs(lam)))
    log_a = (-LRU_C) * r * sp
    a = jnp.exp(log_a)
    t = 1.0 - a * a
    mult = jnp.where(t > 0.0, t * lax.rsqrt(t), 0.0)
    a_ref[...] = a
    b_ref[...] = mult * (ig * xc)

    row = lax.broadcasted_iota(jnp.int32, (SUBLANES, c), 0)

    def group(gi, h_prev):
        r0 = pl.multiple_of(gi * SUBLANES, SUBLANES)
        av = a_ref[pl.ds(r0, SUBLANES), :]
        bv = b_ref[pl.ds(r0, SUBLANES), :]
        d = 1
        while d < SUBLANES:
            a_sh = jnp.where(row >= d, pltpu.roll(av, d, axis=0), 1.0)
            b_sh = jnp.where(row >= d, pltpu.roll(bv, d, axis=0), 0.0)
            bv = bv + av * b_sh
            av = av * a_sh
            d *= 2
        hv = bv + av * h_prev
        b_ref[pl.ds(r0, SUBLANES), :] = hv
        return jnp.broadcast_to(hv[SUBLANES - 1:SUBLANES, :], (SUBLANES, c))

    hc_ref[...] = lax.fori_loop(0, tt // SUBLANES, group, hc_ref[...], unroll=4)

    g = ug_ref[...].astype(F32)
    gelu = 0.5 * g * (1.0 + jnp.tanh(math.sqrt(2.0 / math.pi) * (g + 0.044715 * (g * g * g))))
    y = b_ref[...] * gelu
    y_ref[...] = (y * _rms_scale(y) * go_ref[...]).astype(BF16)


def _rglru(u, conv_w, conv_b, w_gates, b_rg, b_ig, lam, g_out, tt=256):
    s = u.shape[0]
    c = conv_w.shape[1]
    vec = pl.BlockSpec((1, c), lambda i: (0, 0))
    return pl.pallas_call(
        functools.partial(_rglru_kernel, tt=tt, c=c),
        grid=(s // tt,),
        in_specs=[pl.BlockSpec((tt, c), lambda i: (i, 0)),
                  pl.BlockSpec((tt, c), lambda i: (i, 1)),
                  pl.BlockSpec((CONV_WIDTH, c), lambda i: (0, 0)),
                  vec, _resident(w_gates.shape), vec, vec, vec, vec],
        out_specs=pl.BlockSpec((tt, c), lambda i: (i, 0)),
        out_shape=jax.ShapeDtypeStruct((s, c), BF16),
        scratch_shapes=[pltpu.VMEM((tt + SUBLANES, c), F32),
                        pltpu.VMEM((tt, c), F32),
                        pltpu.VMEM((tt, c), F32),
                        pltpu.VMEM((SUBLANES, c), F32)],
        compiler_params=_params("arbitrary"),
        name="rglru",
    )(u, u, conv_w, conv_b, w_gates, b_rg, b_ig, lam, g_out)


def _fox_kernel(r_ref, qt_ref, k_ref, g_ref, vt_ref, o_ref, m_ref, acc_ref, s0_ref, s1_ref, cm_ref, *, tq, tk):
    kpq = tq // tk
    assert tq == kpq * tk and kpq % 2 == 0
    hd = pl.program_id(0)
    qi = pl.program_id(1)
    dh = qt_ref.shape[1]
    n_full = kpq * qi
    row = lax.broadcasted_iota(jnp.int32, (LANES, tq), 0)
    ones_q = jnp.where(row < N_BIAS_PIECES, 1.0, 0.0).astype(BF16)
    qt_aug = jnp.concatenate([qt_ref[0], ones_q], axis=0)
    ones_v = jnp.ones((BF16_ROWS, tk), BF16)
    r_q = r_ref[hd, n_full]
    m_ref[...] = jnp.full_like(m_ref, -jnp.inf)
    acc_ref[...] = jnp.zeros_like(acc_ref)
    s_refs = (s0_ref, s1_ref)

    def scores(kb, diag_j, slot):
        c0 = 0 if diag_j is None else diag_j * tk
        k0 = pl.multiple_of(kb * tk, tk)
        k_aug = jnp.concatenate([k_ref[0, pl.ds(k0, tk), :], g_ref[0, pl.ds(k0, tk), :]], axis=1)
        s = jnp.dot(k_aug, qt_aug[:, c0:], preferred_element_type=F32)
        if diag_j is not None:
            kk = lax.broadcasted_iota(jnp.int32, s.shape, 0)
            qq = lax.broadcasted_iota(jnp.int32, s.shape, 1)
            s = jnp.where(kk <= qq, s, -jnp.inf)
        s_refs[slot][:, c0:] = s
        cm_ref[slot, :, c0:] = jnp.max(s, axis=0, keepdims=True)

    def update(kb, diag_j, slot):
        c0 = 0 if diag_j is None else diag_j * tk
        k0 = pl.multiple_of(kb * tk, tk)
        shift = r_ref[hd, kb] - r_q
        m_prev = m_ref[:, c0:]
        m_new = jnp.maximum(m_prev, cm_ref[slot, :, c0:] + shift)
        alpha = jnp.exp2(m_prev - m_new)
        p = jnp.exp2(s_refs[slot][:, c0:] - (m_new - shift)).astype(BF16)
        vt_aug = jnp.concatenate([vt_ref[0, :, pl.ds(k0, tk)], ones_v], axis=0)
        acc_ref[:, c0:] = alpha * acc_ref[:, c0:] + jnp.dot(vt_aug, p, preferred_element_type=F32)
        m_ref[:, c0:] = m_new

    scores(n_full, 0, 0)
    for j in range(1, kpq):
        scores(n_full + j, j, j % 2)
        update(n_full + j - 1, j - 1, (j - 1) % 2)
    c_last = (kpq - 1) * tk
    s_refs[(kpq - 1) % 2][:, :c_last] = jnp.full((tk, c_last), -jnp.inf, F32)
    cm_ref[(kpq - 1) % 2, :, :c_last] = jnp.full((1, c_last), -jnp.inf, F32)

    def body(u, carry):
        for t in range(kpq):
            blk = kpq * u + t
            scores(blk, None, t % 2)
            prev = jnp.where(u == 0, n_full + kpq - 1, blk - 1) if t == 0 else blk - 1
            update(prev, None, (t - 1) % 2)
        return carry

    lax.fori_loop(0, qi, body, 0)
    update(jnp.where(qi == 0, kpq - 1, n_full - 1), None, (kpq - 1) % 2)
    out_t = acc_ref[0:dh, :] / acc_ref[dh:dh + 1, :]
    o_ref[...] = jnp.transpose(out_t).astype(o_ref.dtype)


def _fox_attention(r, qt, k, gp, vt, tq, tk):
    n_heads, s, dh = k.shape
    kern = functools.partial(_fox_kernel, tq=tq, tk=tk)
    head = lambda h, i: (h, 0, 0)
    return pl.pallas_call(
        kern,
        grid=(n_heads, s // tq),
        in_specs=[pl.BlockSpec(memory_space=pltpu.SMEM),
                  pl.BlockSpec((1, dh, tq), lambda h, i: (h, 0, i)),
                  pl.BlockSpec((1, s, dh), head),
                  pl.BlockSpec((1, s, LANES), head),
                  pl.BlockSpec((1, dh, s), head)],
        out_specs=pl.BlockSpec((tq, dh), lambda h, i: (i, h)),
        out_shape=jax.ShapeDtypeStruct((s, n_heads * dh), BF16),
        scratch_shapes=[pltpu.VMEM((1, tq), F32),
                        pltpu.VMEM((dh + BF16_ROWS, tq), F32),
                        pltpu.VMEM((tk, tq), F32), pltpu.VMEM((tk, tq), F32),
                        pltpu.VMEM((2, 1, tq), F32)],
        compiler_params=_params("arbitrary", "arbitrary"),
        name="fox_attn",
    )(r, qt, k, gp, vt)


def _out_proj_kernel(yl_ref, ya_ref, x_ref, ga_ref, wl_ref, wa_ref, gp_ref, gt_ref,
                     g2_ref, sh_ref, sc_ref, x1_ref, h2_ref):
    ya = ya_ref[...].astype(F32)
    ya = (ya * _rms_scale(ya) * ga_ref[...]).astype(BF16)
    y = jnp.dot(yl_ref[...], wl_ref[...], preferred_element_type=F32)
    y = y + jnp.dot(ya, wa_ref[...], preferred_element_type=F32)
    x1 = x_ref[...] + gt_ref[...] * (y * _rms_scale(y) * gp_ref[...])
    x1_ref[...] = x1
    h2 = x1 * _rms_scale(x1) * g2_ref[...]
    h2_ref[...] = (h2 * (1.0 + sc_ref[...]) + sh_ref[...]).astype(BF16)


def _out_proj(yl, ya, x2d, g_att, w_l, w_a, g_post, gt1, g_pre2, sh2, sc2, tm=512):
    s, d = x2d.shape
    c = yl.shape[1]
    row = lambda i: (i, 0)
    vec_d = pl.BlockSpec((1, d), lambda i: (0, 0))
    vec_c = pl.BlockSpec((1, c), lambda i: (0, 0))
    return pl.pallas_call(
        _out_proj_kernel,
        grid=(s // tm,),
        in_specs=[pl.BlockSpec((tm, c), row), pl.BlockSpec((tm, c), row), pl.BlockSpec((tm, d), row),
                  vec_c, _resident(w_l.shape), _resident(w_a.shape), vec_d, vec_d,
                  vec_d, vec_d, vec_d],
        out_specs=[pl.BlockSpec((tm, d), row), pl.BlockSpec((tm, d), row)],
        out_shape=[jax.ShapeDtypeStruct((s, d), F32), jax.ShapeDtypeStruct((s, d), BF16)],
        compiler_params=_params("arbitrary"),
        name="out_proj",
    )(yl, ya, x2d, g_att, w_l, w_a, g_post, gt1, g_pre2, sh2, sc2)


def _ffn_kernel(h_ref, w1_ref, w2_ref, x_hbm, gp_ref, gt_ref, o_ref, xbuf_ref, xsem):
    i = pl.program_id(0)
    j = pl.program_id(1)
    tm = xbuf_ref.shape[0]

    def residual_copy():
        r0 = pl.multiple_of(i * tm, tm)
        return pltpu.make_async_copy(x_hbm.at[pl.ds(r0, tm), :], xbuf_ref, xsem)

    @pl.when(j == 0)
    def _():
        residual_copy().start()
        o_ref[...] = jnp.zeros_like(o_ref)

    hid = jnp.dot(h_ref[...], w1_ref[...], preferred_element_type=F32)
    hid = jnp.maximum(hid, 0.0)
    hid = (hid * hid).astype(BF16)
    o_ref[...] += jnp.dot(hid, w2_ref[...], preferred_element_type=F32)

    @pl.when(j == pl.num_programs(1) - 1)
    def _():
        residual_copy().wait()
        y = o_ref[...]
        o_ref[...] = xbuf_ref[...] + gt_ref[...] * (y * _rms_scale(y) * gp_ref[...])


def _ffn(h2, w1, w2, x1, g_post, gt2, tm=1024, tf=512):
    s, d = x1.shape
    dff = w1.shape[1]
    vec = pl.BlockSpec((1, d), lambda i, j: (0, 0))
    return pl.pallas_call(
        _ffn_kernel,
        grid=(s // tm, dff // tf),
        in_specs=[pl.BlockSpec((tm, d), lambda i, j: (i, 0)),
                  pl.BlockSpec((d, tf), lambda i, j: (0, j)),
                  pl.BlockSpec((tf, d), lambda i, j: (j, 0)),
                  pl.BlockSpec(memory_space=pl.ANY),
                  vec, vec],
        out_specs=pl.BlockSpec((tm, d), lambda i, j: (i, 0)),
        out_shape=jax.ShapeDtypeStruct((s, d), F32),
        scratch_shapes=[pltpu.VMEM((tm, d), F32), pltpu.SemaphoreType.DMA(())],
        compiler_params=_params("arbitrary", "arbitrary"),
        name="ffn",
    )(h2, w1, w2, x1, g_post, gt2)


def _block_diag(w):
    n, bs, _ = w.shape
    eye = jnp.eye(n, dtype=w.dtype)
    return jnp.einsum("nij,nm->nimj", w, eye).reshape(n * bs, n * bs)


def kernel(x, c, w_ada, b_ada, g_pre_mix, g_post_mix, g_pre_mlp, g_post_mlp, w_in, conv_w, conv_b,
           w_rg, b_rg, w_ig, b_ig, lru_lambda, b_forget, g_lru_out, g_att_out, w_out, w_ff1, w_ff2):
    bsz, s, d = x.shape
    assert bsz == 1, "kernels are written for a single sequence"
    depth = w_ada.shape[0]
    d_lru = conv_w.shape[-1]
    n_heads = b_forget.shape[-1]
    d_att = n_heads * HEAD_DIM
    assert n_heads <= LANES

    x2d = x.reshape(s, d)
    c8 = jnp.broadcast_to(c, (SUBLANES, d))
    for l in range(depth):
        mod = _adaln(c8, w_ada[l], b_ada[l][None, :])[0:1]
        sh1, sc1, gt1, sh2, sc2, gt2 = jnp.split(mod, 6, axis=-1)

        w = w_in[l].astype(BF16)
        wf = jnp.pad(w[:, 2 * d_lru + 3 * d_att:], ((0, 0), (0, LANES - n_heads)))
        u, qt, k, vt, f = _in_proj(x2d, g_pre_mix[l][None], sh1, sc1, w, wf, 2 * d_lru, d_att)

        b_pad = jnp.pad(b_forget[l], (0, LANES - n_heads))[None, :]
        gp, gabs = _fgate(f, b_pad, n_heads, ATT_TK)
        r = jnp.transpose(gabs[::ATT_TK, :n_heads])

        w_gates = jnp.concatenate([_block_diag(w_rg[l]), _block_diag(w_ig[l])], axis=1).astype(BF16)
        y_lru = _rglru(u, conv_w[l], conv_b[l][None], w_gates, b_rg[l][None], b_ig[l][None],
                       lru_lambda[l][None], g_lru_out[l][None])
        y_att = _fox_attention(r, qt, k, gp, vt, ATT_TQ, ATT_TK)

        wo = w_out[l].astype(BF16)
        x2d, h2 = _out_proj(y_lru, y_att, x2d, g_att_out[l][None], wo[:d_lru], wo[d_lru:],
                            g_post_mix[l][None], gt1, g_pre_mlp[l][None], sh2, sc2)
        x2d = _ffn(h2, w_ff1[l].astype(BF16), w_ff2[l].astype(BF16), x2d, g_post_mlp[l][None], gt2)
    return x2d.reshape(bsz, s, d)
```

```python
import functools
import math

import jax
import jax.numpy as jnp
from jax import lax
from jax.experimental import pallas as pl
from jax.experimental.pallas import tpu as pltpu

F32 = jnp.float32
BF16 = jnp.bfloat16

EPS = 1e-6
LRU_C = 8.0
CONV_WIDTH = 4
HEAD_DIM = 128
LOG2E = math.log2(math.e)
LANES = 128
SUBLANES = 8
BF16_ROWS = 16
MXU_TILE = 256
ATT_TK = 512
ATT_TQ = 2048
N_BIAS_PIECES = 3
VMEM_LIMIT_BYTES = 56 * 1024 * 1024


def _params(*sem):
    return pltpu.CompilerParams(dimension_semantics=sem, vmem_limit_bytes=VMEM_LIMIT_BYTES)


def _sigmoid(z):
    return 1.0 / (1.0 + jnp.exp(-z))


def _rms_scale(v):
    return lax.rsqrt(jnp.mean(v * v, axis=-1, keepdims=True) + EPS)


def _resident(shape):
    nd = len(shape)
    return pl.BlockSpec(shape, lambda *_: (0,) * nd, pipeline_mode=pl.Buffered(1))


def _adaln_kernel(c_ref, w_ref, b_ref, o_ref):
    c = c_ref[...]
    ca = c * _sigmoid(c)
    o_ref[...] = jnp.dot(ca, w_ref[...], preferred_element_type=F32,
                         precision=lax.Precision.HIGHEST) + b_ref[...]


def _adaln(c8, w_ada, b_ada, tn=1024):
    d, n = w_ada.shape
    return pl.pallas_call(
        _adaln_kernel,
        grid=(n // tn,),
        in_specs=[pl.BlockSpec((SUBLANES, d), lambda j: (0, 0)),
                  pl.BlockSpec((d, tn), lambda j: (0, j)),
                  pl.BlockSpec((1, tn), lambda j: (0, j))],
        out_specs=pl.BlockSpec((SUBLANES, tn), lambda j: (0, j)),
        out_shape=jax.ShapeDtypeStruct((SUBLANES, n), F32),
        compiler_params=_params("arbitrary"),
        name="adaln",
    )(c8, w_ada, b_ada)


def _in_proj_kernel(x_ref, g_ref, sh_ref, sc_ref, wu_ref, wq_ref, wk_ref, wv_ref, wf_ref,
                    cw_ref, cb_ref, wg_ref, brg_ref, big_ref, lam_ref,
                    a_ref, b_ref, xg_ref, qt_ref, k_ref, vt_ref, f_ref, ext_ref, *, n_heads, q_scale):
    tm = x_ref.shape[0]
    c = a_ref.shape[1]
    pad = SUBLANES

    @pl.when(pl.program_id(0) == 0)
    def _():
        ext_ref[0:pad, :] = jnp.zeros((pad, c), F32)

    x = x_ref[...]
    h = x * _rms_scale(x) * g_ref[...]
    h = (h * (1.0 + sc_ref[...]) + sh_ref[...]).astype(BF16)
    u = jnp.dot(h, wu_ref[...], preferred_element_type=F32)
    ug = u[:, c:]
    gelu = 0.5 * ug * (1.0 + jnp.tanh(math.sqrt(2.0 / math.pi) * (ug + 0.044715 * (ug * ug * ug))))
    xg_ref[...] = gelu.astype(BF16)

    ul = u[:, :c]
    ext_ref[pad:pad + tm, :] = ul
    xc = ul * cw_ref[CONV_WIDTH - 1:CONV_WIDTH, :] + cb_ref[...]
    for kk in range(CONV_WIDTH - 1):
        back = CONV_WIDTH - 1 - kk
        xc = xc + ext_ref[pad - back:pad - back + tm, :] * cw_ref[kk:kk + 1, :]
    ext_ref[0:pad, :] = ul[tm - pad:tm, :]

    gw = wg_ref.shape[1]
    gr, gi = [], []
    for grp in range(c // gw):
        gg = jnp.dot(xc[:, grp * gw:(grp + 1) * gw].astype(BF16), wg_ref[grp], preferred_element_type=F32)
        gr.append(gg[:, :gw])
        gi.append(gg[:, gw:])
    r = 0.5 + 0.5 * jnp.tanh(0.5 * (jnp.concatenate(gr, axis=1) + brg_ref[...]))
    ig = 0.5 + 0.5 * jnp.tanh(0.5 * (jnp.concatenate(gi, axis=1) + big_ref[...]))
    lam = lam_ref[...]
    sp = jnp.maximum(-lam, 0.0) + jnp.log(1.0 + jnp.exp(-jnp.abs(lam)))
    a = jnp.exp((-LRU_C) * r * sp)
    t = 1.0 - a * a
    mult = jnp.where(t > 0.0, t * lax.rsqrt(t), 0.0)
    a_ref[...] = a
    b_ref[...] = mult * (ig * xc)

    q = jnp.dot(h, wq_ref[...], preferred_element_type=F32) * q_scale
    k = jnp.dot(h, wk_ref[...], preferred_element_type=F32)
    v = jnp.dot(h, wv_ref[...], preferred_element_type=F32)
    for hd in range(n_heads):
        sl = slice(hd * HEAD_DIM, (hd + 1) * HEAD_DIM)
        qt_ref[hd] = jnp.transpose(q[:, sl]).astype(BF16)
        k_ref[hd] = k[:, sl].astype(BF16)
        vt_ref[hd] = jnp.transpose(v[:, sl]).astype(BF16)
    f_ref[...] = jnp.dot(h, wf_ref[...], preferred_element_type=F32)


def _in_proj(x2d, g, sh, sc, w, wf, conv_w, conv_b, w_gates, b_rg, b_ig, lam, d_att, tm=512):
    s, d = x2d.shape
    c = conv_w.shape[1]
    n_u = 2 * c
    n_heads = d_att // HEAD_DIM
    assert n_u % d_att == 0
    q_blk = n_u // d_att

    def cols(width, blk):
        return pl.BlockSpec((d, width), lambda i: (0, blk), pipeline_mode=pl.Buffered(1))

    row = lambda i: (i, 0)
    vec = pl.BlockSpec((1, d), lambda i: (0, 0))
    kern = functools.partial(_in_proj_kernel, n_heads=n_heads, q_scale=HEAD_DIM ** -0.5 * LOG2E)
    k_shape = jax.ShapeDtypeStruct((n_heads, s, HEAD_DIM), BF16)
    k_spec = pl.BlockSpec((n_heads, tm, HEAD_DIM), lambda i: (0, i, 0))
    t_shape = jax.ShapeDtypeStruct((n_heads, HEAD_DIM, s), BF16)
    t_spec = pl.BlockSpec((n_heads, HEAD_DIM, tm), lambda i: (0, 0, i))
    vec_c = pl.BlockSpec((1, c), lambda i: (0, 0))
    lru_spec = pl.BlockSpec((tm, c), row)
    return pl.pallas_call(
        kern,
        grid=(s // tm,),
        in_specs=[pl.BlockSpec((tm, d), row), vec, vec, vec,
                  cols(n_u, 0), cols(d_att, q_blk), cols(d_att, q_blk + 1), cols(d_att, q_blk + 2),
                  _resident(wf.shape),
                  pl.BlockSpec((CONV_WIDTH, c), lambda i: (0, 0)), vec_c, _resident(w_gates.shape),
                  vec_c, vec_c, vec_c],
        out_specs=[lru_spec, lru_spec, lru_spec, t_spec, k_spec, t_spec,
                   pl.BlockSpec((tm, LANES), row)],
        out_shape=[jax.ShapeDtypeStruct((s, c), F32), jax.ShapeDtypeStruct((s, c), F32),
                   jax.ShapeDtypeStruct((s, c), BF16), t_shape, k_shape, t_shape,
                   jax.ShapeDtypeStruct((s, LANES), F32)],
        scratch_shapes=[pltpu.VMEM((tm + SUBLANES, c), F32)],
        compiler_params=_params("arbitrary"),
        name="in_proj",
    )(x2d, g, sh, sc, w, w, w, w, wf, conv_w, conv_b, w_gates, b_rg, b_ig, lam)


def _fgate_kernel(f_ref, b_ref, gp_ref, gabs_ref, carry_ref, *, tm, tk, n_heads):
    @pl.when(pl.program_id(0) == 0)
    def _():
        carry_ref[...] = jnp.zeros_like(carry_ref)

    z = f_ref[...] + b_ref[...]
    inc = (jnp.log(1.0 + jnp.exp(-jnp.abs(z))) - jnp.minimum(z, 0.0)) * LOG2E
    row = lax.broadcasted_iota(jnp.int32, (tm, LANES), 0)
    acc = inc
    d = 1
    while d < tm:
        acc = acc + jnp.where(row >= d, pltpu.roll(acc, d, axis=0), 0.0)
        d *= 2
    carry = carry_ref[0:1, :]
    gabs_ref[...] = acc + carry
    carry_ref[...] = jnp.broadcast_to(acc[tm - 1:tm, :] + carry, carry_ref.shape)

    lane = lax.broadcasted_iota(jnp.int32, (tk, LANES), 1)
    for blk in range(tm // tk):
        a = acc[blk * tk:(blk + 1) * tk, :]
        rel = a - a[0:1, :]
        for hd in range(n_heads):
            rem = jnp.broadcast_to(rel[:, hd:hd + 1], (tk, LANES))
            out = jnp.zeros((tk, LANES), F32)
            for piece in range(N_BIAS_PIECES):
                pc = rem.astype(BF16).astype(F32)
                out = jnp.where(lane == piece, pc, out)
                rem = rem - pc
            gp_ref[hd, blk * tk:(blk + 1) * tk, :] = out.astype(BF16)


def _fgate(f, b_pad, n_heads, tk, tm=1024):
    s = f.shape[0]
    return pl.pallas_call(
        functools.partial(_fgate_kernel, tm=tm, tk=tk, n_heads=n_heads),
        grid=(s // tm,),
        in_specs=[pl.BlockSpec((tm, LANES), lambda i: (i, 0)),
                  pl.BlockSpec((1, LANES), lambda i: (0, 0))],
        out_specs=[pl.BlockSpec((n_heads, tm, LANES), lambda i: (0, i, 0)),
                   pl.BlockSpec((tm, LANES), lambda i: (i, 0))],
        out_shape=[jax.ShapeDtypeStruct((n_heads, s, LANES), BF16),
                   jax.ShapeDtypeStruct((s, LANES), F32)],
        scratch_shapes=[pltpu.VMEM((SUBLANES, LANES), F32)],
        compiler_params=_params("arbitrary"),
        name="fgate",
    )(f, b_pad)


def _lru_scan_kernel(a_ref, b_ref, xg_ref, go_ref, y_ref, h_ref, hc_ref, *, tt, c):
    @pl.when(pl.program_id(0) == 0)
    def _():
        hc_ref[...] = jnp.zeros_like(hc_ref)

    row = lax.broadcasted_iota(jnp.int32, (SUBLANES, c), 0)

    def group(gi, h_prev):
        r0 = pl.multiple_of(gi * SUBLANES, SUBLANES)
        av = a_ref[pl.ds(r0, SUBLANES), :]
        bv = b_ref[pl.ds(r0, SUBLANES), :]
        d = 1
        while d < SUBLANES:
            a_sh = jnp.where(row >= d, pltpu.roll(av, d, axis=0), 1.0)
            b_sh = jnp.where(row >= d, pltpu.roll(bv, d, axis=0), 0.0)
            bv = bv + av * b_sh
            av = av * a_sh
            d *= 2
        hv = bv + av * h_prev
        h_ref[pl.ds(r0, SUBLANES), :] = hv
        return jnp.broadcast_to(hv[SUBLANES - 1:SUBLANES, :], (SUBLANES, c))

    hc_ref[...] = lax.fori_loop(0, tt // SUBLANES, group, hc_ref[...], unroll=4)

    y = h_ref[...] * xg_ref[...].astype(F32)
    y_ref[...] = (y * _rms_scale(y) * go_ref[...]).astype(BF16)


def _lru_scan(a, b, xg, g_out, tt=512):
    s, c = a.shape
    blk = pl.BlockSpec((tt, c), lambda i: (i, 0))
    return pl.pallas_call(
        functools.partial(_lru_scan_kernel, tt=tt, c=c),
        grid=(s // tt,),
        in_specs=[blk, blk, blk, pl.BlockSpec((1, c), lambda i: (0, 0))],
        out_specs=blk,
        out_shape=jax.ShapeDtypeStruct((s, c), BF16),
        scratch_shapes=[pltpu.VMEM((tt, c), F32),
                        pltpu.VMEM((SUBLANES, c), F32)],
        compiler_params=_params("arbitrary"),
        name="lru_scan",
    )(a, b, xg, g_out)


def _fox_kernel(r_ref, qt_ref, k_ref, g_ref, vt_ref, o_ref, m_ref, acc_ref, s0_ref, s1_ref, cm_ref, *, tq, tk):
    kpq = tq // tk
    assert tq == kpq * tk and kpq % 2 == 0
    hd = pl.program_id(0)
    qi = pl.program_id(1)
    dh = qt_ref.shape[1]
    n_full = kpq * qi
    row = lax.broadcasted_iota(jnp.int32, (LANES, tq), 0)
    ones_q = jnp.where(row < N_BIAS_PIECES, 1.0, 0.0).astype(BF16)
    qt_aug = jnp.concatenate([qt_ref[0], ones_q], axis=0)
    ones_v = jnp.ones((BF16_ROWS, tk), BF16)
    r_q = r_ref[hd, n_full]
    m_ref[...] = jnp.full_like(m_ref, -jnp.inf)
    acc_ref[...] = jnp.zeros_like(acc_ref)
    s_refs = (s0_ref, s1_ref)

    def scores(kb, diag_j, slot):
        c0 = 0 if diag_j is None else diag_j * tk
        k0 = pl.multiple_of(kb * tk, tk)
        k_aug = jnp.concatenate([k_ref[0, pl.ds(k0, tk), :], g_ref[0, pl.ds(k0, tk), :]], axis=1)
        s = jnp.dot(k_aug, qt_aug[:, c0:], preferred_element_type=F32)
        if diag_j is not None:
            kk = lax.broadcasted_iota(jnp.int32, s.shape, 0)
            qq = lax.broadcasted_iota(jnp.int32, s.shape, 1)
            s = jnp.where(kk <= qq, s, -jnp.inf)
        s_refs[slot][:, c0:] = s
        cm_ref[slot, :, c0:] = jnp.max(s, axis=0, keepdims=True)

    def update(kb, diag_j, slot):
        c0 = 0 if diag_j is None else diag_j * tk
        k0 = pl.multiple_of(kb * tk, tk)
        shift = r_ref[hd, kb] - r_q
        m_prev = m_ref[:, c0:]
        m_new = jnp.maximum(m_prev, cm_ref[slot, :, c0:] + shift)
        alpha = jnp.exp2(m_prev - m_new)
        p = jnp.exp2(s_refs[slot][:, c0:] - (m_new - shift)).astype(BF16)
        vt_aug = jnp.concatenate([vt_ref[0, :, pl.ds(k0, tk)], ones_v], axis=0)
        acc_ref[:, c0:] = alpha * acc_ref[:, c0:] + jnp.dot(vt_aug, p, preferred_element_type=F32)
        m_ref[:, c0:] = m_new

    scores(n_full, 0, 0)
    for j in range(1, kpq):
        scores(n_full + j, j, j % 2)
        update(n_full + j - 1, j - 1, (j - 1) % 2)
    c_last = (kpq - 1) * tk
    s_refs[(kpq - 1) % 2][:, :c_last] = jnp.full((tk, c_last), -jnp.inf, F32)
    cm_ref[(kpq - 1) % 2, :, :c_last] = jnp.full((1, c_last), -jnp.inf, F32)

    def body(u, carry):
        for t in range(kpq):
            blk = kpq * u + t
            scores(blk, None, t % 2)
            prev = jnp.where(u == 0, n_full + kpq - 1, blk - 1) if t == 0 else blk - 1
            update(prev, None, (t - 1) % 2)
        return carry

    lax.fori_loop(0, qi, body, 0)
    update(jnp.where(qi == 0, kpq - 1, n_full - 1), None, (kpq - 1) % 2)
    out_t = acc_ref[0:dh, :] / acc_ref[dh:dh + 1, :]
    o_ref[...] = jnp.transpose(out_t).astype(o_ref.dtype)


def _fox_attention(r, qt, k, gp, vt, tq, tk):
    n_heads, s, dh = k.shape
    kern = functools.partial(_fox_kernel, tq=tq, tk=tk)
    head = lambda h, i: (h, 0, 0)
    return pl.pallas_call(
        kern,
        grid=(n_heads, s // tq),
        in_specs=[pl.BlockSpec(memory_space=pltpu.SMEM),
                  pl.BlockSpec((1, dh, tq), lambda h, i: (h, 0, i)),
                  pl.BlockSpec((1, s, dh), head),
                  pl.BlockSpec((1, s, LANES), head),
                  pl.BlockSpec((1, dh, s), head)],
        out_specs=pl.BlockSpec((tq, dh), lambda h, i: (i, h)),
        out_shape=jax.ShapeDtypeStruct((s, n_heads * dh), BF16),
        scratch_shapes=[pltpu.VMEM((1, tq), F32),
                        pltpu.VMEM((dh + BF16_ROWS, tq), F32),
                        pltpu.VMEM((tk, tq), F32), pltpu.VMEM((tk, tq), F32),
                        pltpu.VMEM((2, 1, tq), F32)],
        compiler_params=_params("arbitrary", "arbitrary"),
        name="fox_attn",
    )(r, qt, k, gp, vt)


def _out_proj_kernel(yl_ref, ya_ref, x_ref, ga_ref, wl_ref, wa_ref, gp_ref, gt_ref,
                     g2_ref, sh_ref, sc_ref, x1_ref, h2_ref):
    ya = ya_ref[...].astype(F32)
    ya = (ya * _rms_scale(ya) * ga_ref[...]).astype(BF16)
    y = jnp.dot(yl_ref[...], wl_ref[...], preferred_element_type=F32)
    y = y + jnp.dot(ya, wa_ref[...], preferred_element_type=F32)
    x1 = x_ref[...] + gt_ref[...] * (y * _rms_scale(y) * gp_ref[...])
    x1_ref[...] = x1
    h2 = x1 * _rms_scale(x1) * g2_ref[...]
    h2_ref[...] = (h2 * (1.0 + sc_ref[...]) + sh_ref[...]).astype(BF16)


def _out_proj(yl, ya, x2d, g_att, w_l, w_a, g_post, gt1, g_pre2, sh2, sc2, tm=512):
    s, d = x2d.shape
    c = yl.shape[1]
    row = lambda i: (i, 0)
    vec_d = pl.BlockSpec((1, d), lambda i: (0, 0))
    vec_c = pl.BlockSpec((1, c), lambda i: (0, 0))
    return pl.pallas_call(
        _out_proj_kernel,
        grid=(s // tm,),
        in_specs=[pl.BlockSpec((tm, c), row), pl.BlockSpec((tm, c), row), pl.BlockSpec((tm, d), row),
                  vec_c, _resident(w_l.shape), _resident(w_a.shape), vec_d, vec_d,
                  vec_d, vec_d, vec_d],
        out_specs=[pl.BlockSpec((tm, d), row), pl.BlockSpec((tm, d), row)],
        out_shape=[jax.ShapeDtypeStruct((s, d), F32), jax.ShapeDtypeStruct((s, d), BF16)],
        compiler_params=_params("arbitrary"),
        name="out_proj",
    )(yl, ya, x2d, g_att, w_l, w_a, g_post, gt1, g_pre2, sh2, sc2)


def _ffn_kernel(h_ref, w1_ref, w2_ref, x_hbm, gp_ref, gt_ref, o_ref, xbuf_ref, xsem):
    i = pl.program_id(0)
    j = pl.program_id(1)
    tm = xbuf_ref.shape[0]

    def residual_copy():
        r0 = pl.multiple_of(i * tm, tm)
        return pltpu.make_async_copy(x_hbm.at[pl.ds(r0, tm), :], xbuf_ref, xsem)

    @pl.when(j == 0)
    def _():
        residual_copy().start()
        o_ref[...] = jnp.zeros_like(o_ref)

    hid = jnp.dot(h_ref[...], w1_ref[...], preferred_element_type=F32)
    hid = jnp.maximum(hid, 0.0)
    hid = (hid * hid).astype(BF16)
    o_ref[...] += jnp.dot(hid, w2_ref[...], preferred_element_type=F32)

    @pl.when(j == pl.num_programs(1) - 1)
    def _():
        residual_copy().wait()
        y = o_ref[...]
        o_ref[...] = xbuf_ref[...] + gt_ref[...] * (y * _rms_scale(y) * gp_ref[...])


def _ffn(h2, w1, w2, x1, g_post, gt2, tm=512, tf=2048):
    s, d = x1.shape
    dff = w1.shape[1]
    vec = pl.BlockSpec((1, d), lambda i, j: (0, 0))
    return pl.pallas_call(
        _ffn_kernel,
        grid=(s // tm, dff // tf),
        in_specs=[pl.BlockSpec((tm, d), lambda i, j: (i, 0)),
                  pl.BlockSpec((d, tf), lambda i, j: (0, j)),
                  pl.BlockSpec((tf, d), lambda i, j: (j, 0)),
                  pl.BlockSpec(memory_space=pl.ANY),
                  vec, vec],
        out_specs=pl.BlockSpec((tm, d), lambda i, j: (i, 0)),
        out_shape=jax.ShapeDtypeStruct((s, d), F32),
        scratch_shapes=[pltpu.VMEM((tm, d), F32), pltpu.SemaphoreType.DMA(())],
        compiler_params=_params("arbitrary", "arbitrary"),
        name="ffn",
    )(h2, w1, w2, x1, g_post, gt2)


def _block_diag(w):
    n, bs, _ = w.shape
    eye = jnp.eye(n, dtype=w.dtype)
    return jnp.einsum("nij,nm->nimj", w, eye).reshape(n * bs, n * bs)


def _gate_weights(w_rg, w_ig):
    n, bs, _ = w_rg.shape
    per = MXU_TILE // bs
    grouped = lambda w: jax.vmap(_block_diag)(w.reshape(n // per, per, bs, bs))
    return jnp.concatenate([grouped(w_rg), grouped(w_ig)], axis=-1).astype(BF16)


def kernel(x, c, w_ada, b_ada, g_pre_mix, g_post_mix, g_pre_mlp, g_post_mlp, w_in, conv_w, conv_b,
           w_rg, b_rg, w_ig, b_ig, lru_lambda, b_forget, g_lru_out, g_att_out, w_out, w_ff1, w_ff2):
    bsz, s, d = x.shape
    assert bsz == 1, "kernels are written for a single sequence"
    depth = w_ada.shape[0]
    d_lru = conv_w.shape[-1]
    n_heads = b_forget.shape[-1]
    d_att = n_heads * HEAD_DIM
    assert n_heads <= LANES

    x2d = x.reshape(s, d)
    c8 = jnp.broadcast_to(c, (SUBLANES, d))
    for l in range(depth):
        mod = _adaln(c8, w_ada[l], b_ada[l][None, :])[0:1]
        sh1, sc1, gt1, sh2, sc2, gt2 = jnp.split(mod, 6, axis=-1)

        w = w_in[l].astype(BF16)
        wf = jnp.pad(w[:, 2 * d_lru + 3 * d_att:], ((0, 0), (0, LANES - n_heads)))
        a, b, xg, qt, k, vt, f = _in_proj(
            x2d, g_pre_mix[l][None], sh1, sc1, w, wf, conv_w[l], conv_b[l][None],
            _gate_weights(w_rg[l], w_ig[l]), b_rg[l][None], b_ig[l][None], lru_lambda[l][None], d_att)

        b_pad = jnp.pad(b_forget[l], (0, LANES - n_heads))[None, :]
        gp, gabs = _fgate(f, b_pad, n_heads, ATT_TK)
        r = jnp.transpose(gabs[::ATT_TK, :n_heads])

        y_lru = _lru_scan(a, b, xg, g_lru_out[l][None])
        y_att = _fox_attention(r, qt, k, gp, vt, ATT_TQ, ATT_TK)

        wo = w_out[l].astype(BF16)
        x2d, h2 = _out_proj(y_lru, y_att, x2d, g_att_out[l][None], wo[:d_lru], wo[d_lru:],
                            g_post_mix[l][None], gt1, g_pre_mlp[l][None], sh2, sc2)
        x2d = _ffn(h2, w_ff1[l].astype(BF16), w_ff2[l].astype(BF16), x2d, g_post_mlp[l][None], gt2)
    return x2d.reshape(bsz, s, d)
```

```python
import functools
import math

import jax
import jax.numpy as jnp
from jax import lax
from jax.experimental import pallas as pl
from jax.experimental.pallas import tpu as pltpu

F32 = jnp.float32
BF16 = jnp.bfloat16

EPS = 1e-6
LRU_C = 8.0
CONV_WIDTH = 4
HEAD_DIM = 128
LOG2E = math.log2(math.e)
LANES = 128
SUBLANES = 8
BF16_ROWS = 16
MXU_TILE = 256
ATT_TK = 512
ATT_TQ = 2048
N_BIAS_PIECES = 3
VMEM_LIMIT_BYTES = 56 * 1024 * 1024


def _params(*sem):
    return pltpu.CompilerParams(dimension_semantics=sem, vmem_limit_bytes=VMEM_LIMIT_BYTES)


def _sigmoid(z):
    return 1.0 / (1.0 + jnp.exp(-z))


def _rms_scale(v):
    return lax.rsqrt(jnp.mean(v * v, axis=-1, keepdims=True) + EPS)


def _resident(shape):
    nd = len(shape)
    return pl.BlockSpec(shape, lambda *_: (0,) * nd, pipeline_mode=pl.Buffered(1))


def _adaln_kernel(c_ref, w_ref, b_ref, o_ref):
    c = c_ref[...]
    ca = c * _sigmoid(c)
    o_ref[...] = jnp.sum(ca * w_ref[...], axis=0, keepdims=True) + b_ref[...]


def _adaln(c_col, w_ada, b_ada, tn=1024):
    d, n = w_ada.shape
    return pl.pallas_call(
        _adaln_kernel,
        grid=(n // tn,),
        in_specs=[pl.BlockSpec((d, 1), lambda j: (0, 0)),
                  pl.BlockSpec((d, tn), lambda j: (0, j)),
                  pl.BlockSpec((1, tn), lambda j: (0, j))],
        out_specs=pl.BlockSpec((1, tn), lambda j: (0, j)),
        out_shape=jax.ShapeDtypeStruct((1, n), F32),
        compiler_params=_params("arbitrary"),
        name="adaln",
    )(c_col, w_ada, b_ada)


def _in_proj_kernel(x_ref, g_ref, sh_ref, sc_ref, wu_ref, wq_ref, wk_ref, wv_ref, wf_ref,
                    cw_ref, cb_ref, wg_ref, brg_ref, big_ref, lam_ref,
                    a_ref, b_ref, xg_ref, qt_ref, k_ref, vt_ref, f_ref, ext_ref, *, n_heads, q_scale):
    tm = x_ref.shape[0]
    c = a_ref.shape[1]
    pad = SUBLANES

    @pl.when(pl.program_id(0) == 0)
    def _():
        ext_ref[0:pad, :] = jnp.zeros((pad, c), F32)

    x = x_ref[...]
    h = x * _rms_scale(x) * g_ref[...]
    h = (h * (1.0 + sc_ref[...]) + sh_ref[...]).astype(BF16)
    u = jnp.dot(h, wu_ref[...], preferred_element_type=F32)
    ug = u[:, c:]
    gelu = 0.5 * ug * (1.0 + jnp.tanh(math.sqrt(2.0 / math.pi) * (ug + 0.044715 * (ug * ug * ug))))
    xg_ref[...] = gelu.astype(BF16)

    ul = u[:, :c]
    ext_ref[pad:pad + tm, :] = ul
    xc = ul * cw_ref[CONV_WIDTH - 1:CONV_WIDTH, :] + cb_ref[...]
    for kk in range(CONV_WIDTH - 1):
        back = CONV_WIDTH - 1 - kk
        xc = xc + ext_ref[pad - back:pad - back + tm, :] * cw_ref[kk:kk + 1, :]
    ext_ref[0:pad, :] = ul[tm - pad:tm, :]

    gw = wg_ref.shape[1]
    gr, gi = [], []
    for grp in range(c // gw):
        gg = jnp.dot(xc[:, grp * gw:(grp + 1) * gw].astype(BF16), wg_ref[grp], preferred_element_type=F32)
        gr.append(gg[:, :gw])
        gi.append(gg[:, gw:])
    r = 0.5 + 0.5 * jnp.tanh(0.5 * (jnp.concatenate(gr, axis=1) + brg_ref[...]))
    ig = 0.5 + 0.5 * jnp.tanh(0.5 * (jnp.concatenate(gi, axis=1) + big_ref[...]))
    lam = lam_ref[...]
    sp = jnp.maximum(-lam, 0.0) + jnp.log(1.0 + jnp.exp(-jnp.abs(lam)))
    a = jnp.exp((-LRU_C) * r * sp)
    t = 1.0 - a * a
    mult = jnp.where(t > 0.0, t * lax.rsqrt(t), 0.0)
    a_ref[...] = a
    b_ref[...] = mult * (ig * xc)

    q = jnp.dot(h, wq_ref[...], preferred_element_type=F32) * q_scale
    k = jnp.dot(h, wk_ref[...], preferred_element_type=F32)
    v = jnp.dot(h, wv_ref[...], preferred_element_type=F32)
    for hd in range(n_heads):
        sl = slice(hd * HEAD_DIM, (hd + 1) * HEAD_DIM)
        qt_ref[hd] = jnp.transpose(q[:, sl]).astype(BF16)
        k_ref[hd] = k[:, sl].astype(BF16)
        vt_ref[hd] = jnp.transpose(v[:, sl]).astype(BF16)
    f_ref[...] = jnp.dot(h, wf_ref[...], preferred_element_type=F32)


def _in_proj(x2d, g, sh, sc, w, wf, conv_w, conv_b, w_gates, b_rg, b_ig, lam, d_att, tm=512):
    s, d = x2d.shape
    c = conv_w.shape[1]
    n_u = 2 * c
    n_heads = d_att // HEAD_DIM
    assert n_u % d_att == 0
    q_blk = n_u // d_att

    def cols(width, blk):
        return pl.BlockSpec((d, width), lambda i: (0, blk), pipeline_mode=pl.Buffered(1))

    row = lambda i: (i, 0)
    vec = pl.BlockSpec((1, d), lambda i: (0, 0))
    kern = functools.partial(_in_proj_kernel, n_heads=n_heads, q_scale=HEAD_DIM ** -0.5 * LOG2E)
    k_shape = jax.ShapeDtypeStruct((n_heads, s, HEAD_DIM), BF16)
    k_spec = pl.BlockSpec((n_heads, tm, HEAD_DIM), lambda i: (0, i, 0))
    t_shape = jax.ShapeDtypeStruct((n_heads, HEAD_DIM, s), BF16)
    t_spec = pl.BlockSpec((n_heads, HEAD_DIM, tm), lambda i: (0, 0, i))
    vec_c = pl.BlockSpec((1, c), lambda i: (0, 0))
    lru_spec = pl.BlockSpec((tm, c), row)
    return pl.pallas_call(
        kern,
        grid=(s // tm,),
        in_specs=[pl.BlockSpec((tm, d), row), vec, vec, vec,
                  cols(n_u, 0), cols(d_att, q_blk), cols(d_att, q_blk + 1), cols(d_att, q_blk + 2),
                  _resident(wf.shape),
                  pl.BlockSpec((CONV_WIDTH, c), lambda i: (0, 0)), vec_c, _resident(w_gates.shape),
                  vec_c, vec_c, vec_c],
        out_specs=[lru_spec, lru_spec, lru_spec, t_spec, k_spec, t_spec,
                   pl.BlockSpec((tm, LANES), row)],
        out_shape=[jax.ShapeDtypeStruct((s, c), F32), jax.ShapeDtypeStruct((s, c), F32),
                   jax.ShapeDtypeStruct((s, c), BF16), t_shape, k_shape, t_shape,
                   jax.ShapeDtypeStruct((s, LANES), F32)],
        scratch_shapes=[pltpu.VMEM((tm + SUBLANES, c), F32)],
        compiler_params=_params("arbitrary"),
        name="in_proj",
    )(x2d, g, sh, sc, w, w, w, w, wf, conv_w, conv_b, w_gates, b_rg, b_ig, lam)


def _fgate_kernel(f_ref, b_ref, gp_ref, gabs_ref, carry_ref, *, tm, tk, n_heads):
    @pl.when(pl.program_id(0) == 0)
    def _():
        carry_ref[...] = jnp.zeros_like(carry_ref)

    z = f_ref[...] + b_ref[...]
    inc = (jnp.log(1.0 + jnp.exp(-jnp.abs(z))) - jnp.minimum(z, 0.0)) * LOG2E
    row = lax.broadcasted_iota(jnp.int32, (tm, LANES), 0)
    acc = inc
    d = 1
    while d < tm:
        acc = acc + jnp.where(row >= d, pltpu.roll(acc, d, axis=0), 0.0)
        d *= 2
    carry = carry_ref[0:1, :]
    gabs_ref[...] = acc + carry
    carry_ref[...] = jnp.broadcast_to(acc[tm - 1:tm, :] + carry, carry_ref.shape)

    lane = lax.broadcasted_iota(jnp.int32, (tk, LANES), 1)
    for blk in range(tm // tk):
        a = acc[blk * tk:(blk + 1) * tk, :]
        rel = a - a[0:1, :]
        for hd in range(n_heads):
            rem = jnp.broadcast_to(rel[:, hd:hd + 1], (tk, LANES))
            out = jnp.zeros((tk, LANES), F32)
            for piece in range(N_BIAS_PIECES):
                pc = rem.astype(BF16).astype(F32)
                out = jnp.where(lane == piece, pc, out)
                rem = rem - pc
            gp_ref[hd, blk * tk:(blk + 1) * tk, :] = out.astype(BF16)


def _fgate(f, b_pad, n_heads, tk, tm=1024):
    s = f.shape[0]
    return pl.pallas_call(
        functools.partial(_fgate_kernel, tm=tm, tk=tk, n_heads=n_heads),
        grid=(s // tm,),
        in_specs=[pl.BlockSpec((tm, LANES), lambda i: (i, 0)),
                  pl.BlockSpec((1, LANES), lambda i: (0, 0))],
        out_specs=[pl.BlockSpec((n_heads, tm, LANES), lambda i: (0, i, 0)),
                   pl.BlockSpec((tm, LANES), lambda i: (i, 0))],
        out_shape=[jax.ShapeDtypeStruct((n_heads, s, LANES), BF16),
                   jax.ShapeDtypeStruct((s, LANES), F32)],
        scratch_shapes=[pltpu.VMEM((SUBLANES, LANES), F32)],
        compiler_params=_params("arbitrary"),
        name="fgate",
    )(f, b_pad)


def _lru_scan_kernel(a_ref, b_ref, xg_ref, go_ref, y_ref, h_ref, hc_ref, *, tt, c):
    @pl.when(pl.program_id(0) == 0)
    def _():
        hc_ref[...] = jnp.zeros_like(hc_ref)

    row = lax.broadcasted_iota(jnp.int32, (SUBLANES, c), 0)

    def group(gi, h_prev):
        r0 = pl.multiple_of(gi * SUBLANES, SUBLANES)
        av = a_ref[pl.ds(r0, SUBLANES), :]
        bv = b_ref[pl.ds(r0, SUBLANES), :]
        d = 1
        while d < SUBLANES:
            a_sh = jnp.where(row >= d, pltpu.roll(av, d, axis=0), 1.0)
            b_sh = jnp.where(row >= d, pltpu.roll(bv, d, axis=0), 0.0)
            bv = bv + av * b_sh
            av = av * a_sh
            d *= 2
        hv = bv + av * h_prev
        h_ref[pl.ds(r0, SUBLANES), :] = hv
        return jnp.broadcast_to(hv[SUBLANES - 1:SUBLANES, :], (SUBLANES, c))

    hc_ref[...] = lax.fori_loop(0, tt // SUBLANES, group, hc_ref[...], unroll=4)

    y = h_ref[...] * xg_ref[...].astype(F32)
    y_ref[...] = (y * _rms_scale(y) * go_ref[...]).astype(BF16)


def _lru_scan(a, b, xg, g_out, tt=512):
    s, c = a.shape
    blk = pl.BlockSpec((tt, c), lambda i: (i, 0))
    return pl.pallas_call(
        functools.partial(_lru_scan_kernel, tt=tt, c=c),
        grid=(s // tt,),
        in_specs=[blk, blk, blk, pl.BlockSpec((1, c), lambda i: (0, 0))],
        out_specs=blk,
        out_shape=jax.ShapeDtypeStruct((s, c), BF16),
        scratch_shapes=[pltpu.VMEM((tt, c), F32),
                        pltpu.VMEM((SUBLANES, c), F32)],
        compiler_params=_params("arbitrary"),
        name="lru_scan",
    )(a, b, xg, g_out)


def _fox_kernel(r_ref, qt_ref, k_ref, g_ref, vt_ref, *rest, tq, tk, n_cast):
    cast_in, (o_ref, *cast_out) = rest[:n_cast], rest[n_cast:2 * n_cast + 1]
    m_ref, acc_ref, s0_ref, s1_ref, cm_ref = rest[2 * n_cast + 1:]
    for w_in_ref, w_out_ref in zip(cast_in, cast_out):
        w_out_ref[...] = w_in_ref[...].astype(BF16)

    kpq = tq // tk
    assert tq == kpq * tk and kpq % 2 == 0
    hd = pl.program_id(0)
    qi = pl.program_id(1)
    dh = qt_ref.shape[1]
    n_full = kpq * qi
    row = lax.broadcasted_iota(jnp.int32, (LANES, tq), 0)
    ones_q = jnp.where(row < N_BIAS_PIECES, 1.0, 0.0).astype(BF16)
    qt_aug = jnp.concatenate([qt_ref[0], ones_q], axis=0)
    ones_v = jnp.ones((BF16_ROWS, tk), BF16)
    r_q = r_ref[hd, n_full]
    m_ref[...] = jnp.full_like(m_ref, -jnp.inf)
    acc_ref[...] = jnp.zeros_like(acc_ref)
    s_refs = (s0_ref, s1_ref)

    def scores(kb, diag_j, slot):
        c0 = 0 if diag_j is None else diag_j * tk
        k0 = pl.multiple_of(kb * tk, tk)
        k_aug = jnp.concatenate([k_ref[0, pl.ds(k0, tk), :], g_ref[0, pl.ds(k0, tk), :]], axis=1)
        s = jnp.dot(k_aug, qt_aug[:, c0:], preferred_element_type=F32)
        if diag_j is not None:
            kk = lax.broadcasted_iota(jnp.int32, s.shape, 0)
            qq = lax.broadcasted_iota(jnp.int32, s.shape, 1)
            s = jnp.where(kk <= qq, s, -jnp.inf)
        s_refs[slot][:, c0:] = s
        cm_ref[slot, :, c0:] = jnp.max(s, axis=0, keepdims=True)

    def update(kb, diag_j, slot):
        c0 = 0 if diag_j is None else diag_j * tk
        k0 = pl.multiple_of(kb * tk, tk)
        shift = r_ref[hd, kb] - r_q
        m_prev = m_ref[:, c0:]
        m_new = jnp.maximum(m_prev, cm_ref[slot, :, c0:] + shift)
        alpha = jnp.exp2(m_prev - m_new)
        p = jnp.exp2(s_refs[slot][:, c0:] - (m_new - shift)).astype(BF16)
        vt_aug = jnp.concatenate([vt_ref[0, :, pl.ds(k0, tk)], ones_v], axis=0)
        acc_ref[:, c0:] = alpha * acc_ref[:, c0:] + jnp.dot(vt_aug, p, preferred_element_type=F32)
        m_ref[:, c0:] = m_new

    pre = [kpq - 1] + [j for j in range(1, kpq - 1) if j != kpq // 2] + [0]
    post = [kpq // 2] if kpq > 2 else []
    n_pre = len(pre)
    scores(n_full + pre[0], pre[0], 0)
    for p in range(1, n_pre):
        scores(n_full + pre[p], pre[p], p % 2)
        update(n_full + pre[p - 1], pre[p - 1], (p - 1) % 2)

    def body(u, carry):
        for t in range(kpq):
            blk = kpq * u + t
            scores(blk, None, (n_pre + t) % 2)
            prev = jnp.where(u == 0, n_full, blk - 1) if t == 0 else blk - 1
            update(prev, None, (n_pre + t - 1) % 2)
        return carry

    lax.fori_loop(0, qi, body, 0)
    pending = (jnp.where(qi == 0, n_full, n_full - 1), None, (n_pre - 1) % 2)
    for i, j in enumerate(post):
        scores(n_full + j, j, (n_pre + i) % 2)
        update(*pending)
        pending = (n_full + j, j, (n_pre + i) % 2)
    update(*pending)
    out_t = acc_ref[0:dh, :] / acc_ref[dh:dh + 1, :]
    o_ref[...] = jnp.transpose(out_t).astype(o_ref.dtype)


def _fox_attention(r, qt, k, gp, vt, tq, tk, cast_along=()):
    n_heads, s, dh = k.shape
    nq = s // tq
    kern = functools.partial(_fox_kernel, tq=tq, tk=tk, n_cast=len(cast_along))
    head = lambda h, i: (h, 0, 0)
    cast_specs = []
    for wgt in cast_along:
        rows = wgt.shape[0] // (n_heads * nq)
        assert rows * n_heads * nq == wgt.shape[0] and rows % BF16_ROWS == 0
        cast_specs.append(pl.BlockSpec((rows, wgt.shape[1]), lambda h, i: (h * nq + i, 0)))
    return pl.pallas_call(
        kern,
        grid=(n_heads, nq),
        in_specs=[pl.BlockSpec(memory_space=pltpu.SMEM),
                  pl.BlockSpec((1, dh, tq), lambda h, i: (h, 0, i)),
                  pl.BlockSpec((1, s, dh), head),
                  pl.BlockSpec((1, s, LANES), head),
                  pl.BlockSpec((1, dh, s), head)] + cast_specs,
        out_specs=[pl.BlockSpec((tq, dh), lambda h, i: (i, h))] + cast_specs,
        out_shape=[jax.ShapeDtypeStruct((s, n_heads * dh), BF16)]
        + [jax.ShapeDtypeStruct(wgt.shape, BF16) for wgt in cast_along],
        scratch_shapes=[pltpu.VMEM((1, tq), F32),
                        pltpu.VMEM((dh + BF16_ROWS, tq), F32),
                        pltpu.VMEM((tk, tq), F32), pltpu.VMEM((tk, tq), F32),
                        pltpu.VMEM((2, 1, tq), F32)],
        compiler_params=_params("arbitrary", "arbitrary"),
        name="fox_attn",
    )(r, qt, k, gp, vt, *cast_along)


def _out_proj_kernel(yl_ref, ya_ref, x_ref, ga_ref, wl_ref, wa_ref, gp_ref, gt_ref,
                     g2_ref, sh_ref, sc_ref, x1_ref, h2_ref):
    ya = ya_ref[...].astype(F32)
    ya = (ya * _rms_scale(ya) * ga_ref[...]).astype(BF16)
    y = jnp.dot(yl_ref[...], wl_ref[...], preferred_element_type=F32)
    y = y + jnp.dot(ya, wa_ref[...], preferred_element_type=F32)
    x1 = x_ref[...] + gt_ref[...] * (y * _rms_scale(y) * gp_ref[...])
    x1_ref[...] = x1
    h2 = x1 * _rms_scale(x1) * g2_ref[...]
    h2_ref[...] = (h2 * (1.0 + sc_ref[...]) + sh_ref[...]).astype(BF16)


def _out_proj(yl, ya, x2d, g_att, w_o, g_post, gt1, g_pre2, sh2, sc2, tm=512):
    s, d = x2d.shape
    c = yl.shape[1]
    assert ya.shape[1] == c and w_o.shape == (2 * c, d)
    row = lambda i: (i, 0)
    vec_d = pl.BlockSpec((1, d), lambda i: (0, 0))
    vec_c = pl.BlockSpec((1, c), lambda i: (0, 0))
    half = lambda blk: pl.BlockSpec((c, d), lambda i: (blk, 0), pipeline_mode=pl.Buffered(1))
    return pl.pallas_call(
        _out_proj_kernel,
        grid=(s // tm,),
        in_specs=[pl.BlockSpec((tm, c), row), pl.BlockSpec((tm, c), row), pl.BlockSpec((tm, d), row),
                  vec_c, half(0), half(1), vec_d, vec_d,
                  vec_d, vec_d, vec_d],
        out_specs=[pl.BlockSpec((tm, d), row), pl.BlockSpec((tm, d), row)],
        out_shape=[jax.ShapeDtypeStruct((s, d), F32), jax.ShapeDtypeStruct((s, d), BF16)],
        compiler_params=_params("arbitrary"),
        name="out_proj",
    )(yl, ya, x2d, g_att, w_o, w_o, g_post, gt1, g_pre2, sh2, sc2)


def _ffn_kernel(h_ref, w1_ref, w2_ref, x_hbm, gp_ref, gt_ref, o_ref, xbuf_ref, xsem):
    i = pl.program_id(0)
    j = pl.program_id(1)
    tm = xbuf_ref.shape[0]

    def residual_copy():
        r0 = pl.multiple_of(i * tm, tm)
        return pltpu.make_async_copy(x_hbm.at[pl.ds(r0, tm), :], xbuf_ref, xsem)

    @pl.when(j == 0)
    def _():
        residual_copy().start()
        o_ref[...] = jnp.zeros_like(o_ref)

    hid = jnp.dot(h_ref[...], w1_ref[...], preferred_element_type=F32)
    hid = jnp.maximum(hid, 0.0)
    hid = (hid * hid).astype(BF16)
    o_ref[...] += jnp.dot(hid, w2_ref[...], preferred_element_type=F32)

    @pl.when(j == pl.num_programs(1) - 1)
    def _():
        residual_copy().wait()
        y = o_ref[...]
        o_ref[...] = xbuf_ref[...] + gt_ref[...] * (y * _rms_scale(y) * gp_ref[...])


def _ffn(h2, w1, w2, x1, g_post, gt2, tm=512, tf=2048):
    s, d = x1.shape
    dff = w1.shape[1]
    vec = pl.BlockSpec((1, d), lambda i, j: (0, 0))
    return pl.pallas_call(
        _ffn_kernel,
        grid=(s // tm, dff // tf),
        in_specs=[pl.BlockSpec((tm, d), lambda i, j: (i, 0)),
                  pl.BlockSpec((d, tf), lambda i, j: (0, j)),
                  pl.BlockSpec((tf, d), lambda i, j: (j, 0)),
                  pl.BlockSpec(memory_space=pl.ANY),
                  vec, vec],
        out_specs=pl.BlockSpec((tm, d), lambda i, j: (i, 0)),
        out_shape=jax.ShapeDtypeStruct((s, d), F32),
        scratch_shapes=[pltpu.VMEM((tm, d), F32), pltpu.SemaphoreType.DMA(())],
        compiler_params=_params("arbitrary", "arbitrary"),
        name="ffn",
    )(h2, w1, w2, x1, g_post, gt2)


def _block_diag(w):
    n, bs, _ = w.shape
    eye = jnp.eye(n, dtype=w.dtype)
    return jnp.einsum("nij,nm->nimj", w, eye).reshape(n * bs, n * bs)


def _gate_weights(w_rg, w_ig):
    n, bs, _ = w_rg.shape
    per = MXU_TILE // bs
    grouped = lambda w: jax.vmap(_block_diag)(w.reshape(n // per, per, bs, bs))
    return jnp.concatenate([grouped(w_rg), grouped(w_ig)], axis=-1).astype(BF16)


def kernel(x, c, w_ada, b_ada, g_pre_mix, g_post_mix, g_pre_mlp, g_post_mlp, w_in, conv_w, conv_b,
           w_rg, b_rg, w_ig, b_ig, lru_lambda, b_forget, g_lru_out, g_att_out, w_out, w_ff1, w_ff2):
    bsz, s, d = x.shape
    assert bsz == 1, "kernels are written for a single sequence"
    depth = w_ada.shape[0]
    d_lru = conv_w.shape[-1]
    n_heads = b_forget.shape[-1]
    d_att = n_heads * HEAD_DIM
    assert n_heads <= LANES

    x2d = x.reshape(s, d)
    c_col = c.reshape(d, 1)
    n_main = 2 * d_lru + 3 * d_att
    for l in range(depth):
        mod = _adaln(c_col, w_ada[l], b_ada[l][None, :])
        sh1, sc1, gt1, sh2, sc2, gt2 = jnp.split(mod, 6, axis=-1)

        w = w_in[l][:, :n_main].astype(BF16)
        wf = jnp.pad(w_in[l][:, n_main:].astype(BF16), ((0, 0), (0, LANES - n_heads)))
        a, b, xg, qt, k, vt, f = _in_proj(
            x2d, g_pre_mix[l][None], sh1, sc1, w, wf, conv_w[l], conv_b[l][None],
            _gate_weights(w_rg[l], w_ig[l]), b_rg[l][None], b_ig[l][None], lru_lambda[l][None], d_att)

        b_pad = jnp.pad(b_forget[l], (0, LANES - n_heads))[None, :]
        gp, gabs = _fgate(f, b_pad, n_heads, ATT_TK)
        r = jnp.transpose(gabs[::ATT_TK, :n_heads])

        y_lru = _lru_scan(a, b, xg, g_lru_out[l][None])
        y_att, wo, w1, w2 = _fox_attention(r, qt, k, gp, vt, ATT_TQ, ATT_TK,
                                           cast_along=(w_out[l], w_ff1[l], w_ff2[l]))
        x2d, h2 = _out_proj(y_lru, y_att, x2d, g_att_out[l][None], wo,
                            g_post_mix[l][None], gt1, g_pre_mlp[l][None], sh2, sc2)
        x2d = _ffn(h2, w1, w2, x2d, g_post_mlp[l][None], gt2)
    return x2d.reshape(bsz, s, d)
```

```python
import functools
import math

import jax
import jax.numpy as jnp
from jax import lax
from jax.experimental import pallas as pl
from jax.experimental.pallas import tpu as pltpu

F32 = jnp.float32
BF16 = jnp.bfloat16

EPS = 1e-6
LRU_C = 8.0
CONV_WIDTH = 4
HEAD_DIM = 128
LOG2E = math.log2(math.e)
LANES = 128
SUBLANES = 8
BF16_ROWS = 16
MXU_TILE = 256
ATT_TK = 512
ATT_TQ = 2048
N_BIAS_PIECES = 3
VMEM_LIMIT_BYTES = 56 * 1024 * 1024


def _params(*sem):
    return pltpu.CompilerParams(dimension_semantics=sem, vmem_limit_bytes=VMEM_LIMIT_BYTES)


def _sigmoid(z):
    return 1.0 / (1.0 + jnp.exp(-z))


def _rms_scale(v):
    return lax.rsqrt(jnp.mean(v * v, axis=-1, keepdims=True) + EPS)


def _resident(shape):
    nd = len(shape)
    return pl.BlockSpec(shape, lambda *_: (0,) * nd, pipeline_mode=pl.Buffered(1))


def _adaln_kernel(c_ref, w_ref, b_ref, o_ref):
    c = c_ref[...]
    ca = c * _sigmoid(c)
    o_ref[...] = jnp.sum(ca * w_ref[...], axis=0, keepdims=True) + b_ref[...]


def _adaln(c_col, w_ada, b_ada, tn=1024):
    d, n = w_ada.shape
    return pl.pallas_call(
        _adaln_kernel,
        grid=(n // tn,),
        in_specs=[pl.BlockSpec((d, 1), lambda j: (0, 0)),
                  pl.BlockSpec((d, tn), lambda j: (0, j)),
                  pl.BlockSpec((1, tn), lambda j: (0, j))],
        out_specs=pl.BlockSpec((1, tn), lambda j: (0, j)),
        out_shape=jax.ShapeDtypeStruct((1, n), F32),
        compiler_params=_params("arbitrary"),
        name="adaln",
    )(c_col, w_ada, b_ada)


def _cast_w_in_kernel(w_ref, wm_ref, wf_ref):
    n_main = wm_ref.shape[1]
    w = w_ref[...]
    wm_ref[...] = w[:, :n_main].astype(BF16)
    wf_ref[...] = jnp.zeros_like(wf_ref)
    wf_ref[:, :w.shape[1] - n_main] = w[:, n_main:].astype(BF16)


def _cast_w_in(w_in, n_main, tr=256):
    d, n = w_in.shape
    assert n - n_main <= LANES
    return pl.pallas_call(
        _cast_w_in_kernel,
        grid=(d // tr,),
        in_specs=[pl.BlockSpec((tr, n), lambda i: (i, 0))],
        out_specs=[pl.BlockSpec((tr, n_main), lambda i: (i, 0)), pl.BlockSpec((tr, LANES), lambda i: (i, 0))],
        out_shape=[jax.ShapeDtypeStruct((d, n_main), BF16), jax.ShapeDtypeStruct((d, LANES), BF16)],
        compiler_params=_params("arbitrary"),
        name="cast_w_in",
    )(w_in)


def _in_proj_kernel(x_ref, g_ref, sh_ref, sc_ref, wu_ref, wq_ref, wk_ref, wv_ref, wf_ref,
                    cw_ref, cb_ref, wg_ref, brg_ref, big_ref, lam_ref,
                    a_ref, b_ref, xg_ref, qt_ref, k_ref, vt_ref, f_ref, ext_ref, *, n_heads, q_scale):
    tm = x_ref.shape[0]
    c = a_ref.shape[1]
    pad = SUBLANES

    @pl.when(pl.program_id(0) == 0)
    def _():
        ext_ref[0:pad, :] = jnp.zeros((pad, c), F32)

    x = x_ref[...]
    h = ((x * _rms_scale(x)) * (g_ref[...] * (1.0 + sc_ref[...])) + sh_ref[...]).astype(BF16)
    u = jnp.dot(h, wu_ref[...], preferred_element_type=F32)
    ug = u[:, c:]
    gelu = 0.5 * ug * (1.0 + jnp.tanh(math.sqrt(2.0 / math.pi) * (ug + 0.044715 * (ug * ug * ug))))
    xg_ref[...] = gelu.astype(BF16)

    ul = u[:, :c]
    ext_ref[pad:pad + tm, :] = ul
    xc = ul * cw_ref[CONV_WIDTH - 1:CONV_WIDTH, :] + cb_ref[...]
    for kk in range(CONV_WIDTH - 1):
        back = CONV_WIDTH - 1 - kk
        xc = xc + ext_ref[pad - back:pad - back + tm, :] * cw_ref[kk:kk + 1, :]
    ext_ref[0:pad, :] = ul[tm - pad:tm, :]

    gw = wg_ref.shape[1]
    gr, gi = [], []
    for grp in range(c // gw):
        gg = jnp.dot(xc[:, grp * gw:(grp + 1) * gw].astype(BF16), wg_ref[grp], preferred_element_type=F32)
        gr.append(gg[:, :gw])
        gi.append(gg[:, gw:])
    r = 0.5 + 0.5 * jnp.tanh(0.5 * (jnp.concatenate(gr, axis=1) + brg_ref[...]))
    ig = 0.5 + 0.5 * jnp.tanh(0.5 * (jnp.concatenate(gi, axis=1) + big_ref[...]))
    lam = lam_ref[...]
    sp = jnp.maximum(-lam, 0.0) + jnp.log(1.0 + jnp.exp(-jnp.abs(lam)))
    a = jnp.exp((-LRU_C) * r * sp)
    t = 1.0 - a * a
    mult = jnp.where(t > 0.0, t * lax.rsqrt(t), 0.0)
    a_ref[...] = a
    b_ref[...] = mult * (ig * xc)

    q = jnp.dot(h, wq_ref[...], preferred_element_type=F32) * q_scale
    k = jnp.dot(h, wk_ref[...], preferred_element_type=F32)
    v = jnp.dot(h, wv_ref[...], preferred_element_type=F32)
    for hd in range(n_heads):
        sl = slice(hd * HEAD_DIM, (hd + 1) * HEAD_DIM)
        qt_ref[hd] = jnp.transpose(q[:, sl]).astype(BF16)
        k_ref[hd] = k[:, sl].astype(BF16)
        vt_ref[hd] = jnp.transpose(v[:, sl]).astype(BF16)
    f_ref[...] = jnp.dot(h, wf_ref[...], preferred_element_type=F32)


def _in_proj(x2d, g, sh, sc, w, wf, conv_w, conv_b, w_gates, b_rg, b_ig, lam, d_att, tm=512):
    s, d = x2d.shape
    c = conv_w.shape[1]
    n_u = 2 * c
    n_heads = d_att // HEAD_DIM
    assert n_u % d_att == 0
    q_blk = n_u // d_att

    def cols(width, blk):
        return pl.BlockSpec((d, width), lambda i: (0, blk), pipeline_mode=pl.Buffered(1))

    row = lambda i: (i, 0)
    vec = pl.BlockSpec((1, d), lambda i: (0, 0))
    kern = functools.partial(_in_proj_kernel, n_heads=n_heads, q_scale=HEAD_DIM ** -0.5 * LOG2E)
    k_shape = jax.ShapeDtypeStruct((n_heads, s, HEAD_DIM), BF16)
    k_spec = pl.BlockSpec((n_heads, tm, HEAD_DIM), lambda i: (0, i, 0))
    t_shape = jax.ShapeDtypeStruct((n_heads, HEAD_DIM, s), BF16)
    t_spec = pl.BlockSpec((n_heads, HEAD_DIM, tm), lambda i: (0, 0, i))
    vec_c = pl.BlockSpec((1, c), lambda i: (0, 0))
    lru_spec = pl.BlockSpec((tm, c), row)
    return pl.pallas_call(
        kern,
        grid=(s // tm,),
        in_specs=[pl.BlockSpec((tm, d), row), vec, vec, vec,
                  cols(n_u, 0), cols(d_att, q_blk), cols(d_att, q_blk + 1), cols(d_att, q_blk + 2),
                  _resident(wf.shape),
                  pl.BlockSpec((CONV_WIDTH, c), lambda i: (0, 0)), vec_c, _resident(w_gates.shape),
                  vec_c, vec_c, vec_c],
        out_specs=[lru_spec, lru_spec, lru_spec, t_spec, k_spec, t_spec,
                   pl.BlockSpec((tm, LANES), row)],
        out_shape=[jax.ShapeDtypeStruct((s, c), F32), jax.ShapeDtypeStruct((s, c), F32),
                   jax.ShapeDtypeStruct((s, c), BF16), t_shape, k_shape, t_shape,
                   jax.ShapeDtypeStruct((s, LANES), F32)],
        scratch_shapes=[pltpu.VMEM((tm + SUBLANES, c), F32)],
        compiler_params=_params("arbitrary"),
        name="in_proj",
    )(x2d, g, sh, sc, w, w, w, w, wf, conv_w, conv_b, w_gates, b_rg, b_ig, lam)


def _fgate_kernel(f_ref, b_ref, gp_ref, gabs_ref, carry_ref, *, tm, tk, n_heads):
    @pl.when(pl.program_id(0) == 0)
    def _():
        carry_ref[...] = jnp.zeros_like(carry_ref)

    z = f_ref[...] + b_ref[...]
    inc = (jnp.log(1.0 + jnp.exp(-jnp.abs(z))) - jnp.minimum(z, 0.0)) * LOG2E
    row = lax.broadcasted_iota(jnp.int32, (tm, LANES), 0)
    acc = inc
    d = 1
    while d < tm:
        acc = acc + jnp.where(row >= d, pltpu.roll(acc, d, axis=0), 0.0)
        d *= 2
    carry = carry_ref[0:1, :]
    gabs_ref[...] = acc + carry
    carry_ref[...] = jnp.broadcast_to(acc[tm - 1:tm, :] + carry, carry_ref.shape)

    lane = lax.broadcasted_iota(jnp.int32, (tk, LANES), 1)
    for blk in range(tm // tk):
        a = acc[blk * tk:(blk + 1) * tk, :]
        rel = a - a[0:1, :]
        for hd in range(n_heads):
            rem = jnp.broadcast_to(rel[:, hd:hd + 1], (tk, LANES))
            out = jnp.zeros((tk, LANES), F32)
            for piece in range(N_BIAS_PIECES):
                pc = rem.astype(BF16).astype(F32)
                out = jnp.where(lane == piece, pc, out)
                rem = rem - pc
            gp_ref[hd, blk * tk:(blk + 1) * tk, :] = out.astype(BF16)


def _fgate(f, b_pad, n_heads, tk, tm=1024):
    s = f.shape[0]
    return pl.pallas_call(
        functools.partial(_fgate_kernel, tm=tm, tk=tk, n_heads=n_heads),
        grid=(s // tm,),
        in_specs=[pl.BlockSpec((tm, LANES), lambda i: (i, 0)),
                  pl.BlockSpec((1, LANES), lambda i: (0, 0))],
        out_specs=[pl.BlockSpec((n_heads, tm, LANES), lambda i: (0, i, 0)),
                   pl.BlockSpec((tm, LANES), lambda i: (i, 0))],
        out_shape=[jax.ShapeDtypeStruct((n_heads, s, LANES), BF16),
                   jax.ShapeDtypeStruct((s, LANES), F32)],
        scratch_shapes=[pltpu.VMEM((SUBLANES, LANES), F32)],
        compiler_params=_params("arbitrary"),
        name="fgate",
    )(f, b_pad)


def _lru_scan_kernel(a_ref, b_ref, xg_ref, go_ref, y_ref, h_ref, hc_ref, *, tt, c):
    @pl.when(pl.program_id(0) == 0)
    def _():
        hc_ref[...] = jnp.zeros_like(hc_ref)

    row = lax.broadcasted_iota(jnp.int32, (SUBLANES, c), 0)

    def group(gi, h_prev):
        r0 = pl.multiple_of(gi * SUBLANES, SUBLANES)
        av = a_ref[pl.ds(r0, SUBLANES), :]
        bv = b_ref[pl.ds(r0, SUBLANES), :]
        d = 1
        while d < SUBLANES:
            a_sh = jnp.where(row >= d, pltpu.roll(av, d, axis=0), 1.0)
            b_sh = jnp.where(row >= d, pltpu.roll(bv, d, axis=0), 0.0)
            bv = bv + av * b_sh
            av = av * a_sh
            d *= 2
        hv = bv + av * h_prev
        h_ref[pl.ds(r0, SUBLANES), :] = hv
        return jnp.broadcast_to(hv[SUBLANES - 1:SUBLANES, :], (SUBLANES, c))

    hc_ref[...] = lax.fori_loop(0, tt // SUBLANES, group, hc_ref[...], unroll=4)

    y = h_ref[...] * xg_ref[...].astype(F32)
    y_ref[...] = (y * _rms_scale(y) * go_ref[...]).astype(BF16)


def _lru_scan(a, b, xg, g_out, tt=1024):
    s, c = a.shape
    blk = pl.BlockSpec((tt, c), lambda i: (i, 0))
    return pl.pallas_call(
        functools.partial(_lru_scan_kernel, tt=tt, c=c),
        grid=(s // tt,),
        in_specs=[blk, blk, blk, pl.BlockSpec((1, c), lambda i: (0, 0))],
        out_specs=blk,
        out_shape=jax.ShapeDtypeStruct((s, c), BF16),
        scratch_shapes=[pltpu.VMEM((tt, c), F32),
                        pltpu.VMEM((SUBLANES, c), F32)],
        compiler_params=_params("arbitrary"),
        name="lru_scan",
    )(a, b, xg, g_out)


def _fox_kernel(r_ref, qt_ref, k_ref, g_ref, vt_ref, *rest, tq, tk, n_cast):
    cast_in, (o_ref, *cast_out) = rest[:n_cast], rest[n_cast:2 * n_cast + 1]
    m_ref, acc_ref, s0_ref, s1_ref, cm_ref = rest[2 * n_cast + 1:]
    for w_in_ref, w_out_ref in zip(cast_in, cast_out):
        w_out_ref[...] = w_in_ref[...].astype(BF16)

    kpq = tq // tk
    assert tq == kpq * tk and kpq % 2 == 0
    hd = pl.program_id(0)
    qi = pl.program_id(1)
    dh = qt_ref.shape[1]
    n_full = kpq * qi
    row = lax.broadcasted_iota(jnp.int32, (LANES, tq), 0)
    ones_q = jnp.where(row < N_BIAS_PIECES, 1.0, 0.0).astype(BF16)
    qt_aug = jnp.concatenate([qt_ref[0], ones_q], axis=0)
    ones_v = jnp.ones((BF16_ROWS, tk), BF16)
    r_q = r_ref[hd, n_full]
    m_ref[...] = jnp.full_like(m_ref, -jnp.inf)
    acc_ref[...] = jnp.zeros_like(acc_ref)
    s_refs = (s0_ref, s1_ref)

    def scores(kb, diag_j, slot):
        c0 = 0 if diag_j is None else diag_j * tk
        k0 = pl.multiple_of(kb * tk, tk)
        k_aug = jnp.concatenate([k_ref[0, pl.ds(k0, tk), :], g_ref[0, pl.ds(k0, tk), :]], axis=1)
        s = jnp.dot(k_aug, qt_aug[:, c0:], preferred_element_type=F32)
        if diag_j is not None:
            kk = lax.broadcasted_iota(jnp.int32, s.shape, 0)
            qq = lax.broadcasted_iota(jnp.int32, s.shape, 1)
            s = jnp.where(kk <= qq, s, -jnp.inf)
        s_refs[slot][:, c0:] = s
        cm_ref[slot, :, c0:] = jnp.max(s, axis=0, keepdims=True)

    def update(kb, diag_j, slot):
        c0 = 0 if diag_j is None else diag_j * tk
        k0 = pl.multiple_of(kb * tk, tk)
        shift = r_ref[hd, kb] - r_q
        m_prev = m_ref[:, c0:]
        m_new = jnp.maximum(m_prev, cm_ref[slot, :, c0:] + shift)
        alpha = jnp.exp2(m_prev - m_new)
        p = jnp.exp2(s_refs[slot][:, c0:] - (m_new - shift)).astype(BF16)
        vt_aug = jnp.concatenate([vt_ref[0, :, pl.ds(k0, tk)], ones_v], axis=0)
        acc_ref[:, c0:] = alpha * acc_ref[:, c0:] + jnp.dot(vt_aug, p, preferred_element_type=F32)
        m_ref[:, c0:] = m_new

    pre = [kpq - 1] + [j for j in range(1, kpq - 1) if j != kpq // 2] + [0]
    post = [kpq // 2] if kpq > 2 else []
    n_pre = len(pre)
    scores(n_full + pre[0], pre[0], 0)
    for p in range(1, n_pre):
        scores(n_full + pre[p], pre[p], p % 2)
        update(n_full + pre[p - 1], pre[p - 1], (p - 1) % 2)

    def body(u, carry):
        for t in range(kpq):
            blk = kpq * u + t
            scores(blk, None, (n_pre + t) % 2)
            prev = jnp.where(u == 0, n_full, blk - 1) if t == 0 else blk - 1
            update(prev, None, (n_pre + t - 1) % 2)
        return carry

    lax.fori_loop(0, qi, body, 0)
    pending = (jnp.where(qi == 0, n_full, n_full - 1), None, (n_pre - 1) % 2)
    for i, j in enumerate(post):
        scores(n_full + j, j, (n_pre + i) % 2)
        update(*pending)
        pending = (n_full + j, j, (n_pre + i) % 2)
    update(*pending)
    out_t = acc_ref[0:dh, :] / acc_ref[dh:dh + 1, :]
    o_ref[...] = jnp.transpose(out_t).astype(o_ref.dtype)


def _fox_attention(r, qt, k, gp, vt, tq, tk, cast_along=()):
    n_heads, s, dh = k.shape
    nq = s // tq
    kern = functools.partial(_fox_kernel, tq=tq, tk=tk, n_cast=len(cast_along))
    head = lambda h, i: (h, 0, 0)
    cast_specs = []
    for wgt in cast_along:
        rows = wgt.shape[0] // (n_heads * nq)
        assert rows * n_heads * nq == wgt.shape[0] and rows % BF16_ROWS == 0
        cast_specs.append(pl.BlockSpec((rows, wgt.shape[1]), lambda h, i: (h * nq + i, 0)))
    return pl.pallas_call(
        kern,
        grid=(n_heads, nq),
        in_specs=[pl.BlockSpec(memory_space=pltpu.SMEM),
                  pl.BlockSpec((1, dh, tq), lambda h, i: (h, 0, i)),
                  pl.BlockSpec((1, s, dh), head),
                  pl.BlockSpec((1, s, LANES), head),
                  pl.BlockSpec((1, dh, s), head)] + cast_specs,
        out_specs=[pl.BlockSpec((tq, dh), lambda h, i: (i, h))] + cast_specs,
        out_shape=[jax.ShapeDtypeStruct((s, n_heads * dh), BF16)]
        + [jax.ShapeDtypeStruct(wgt.shape, BF16) for wgt in cast_along],
        scratch_shapes=[pltpu.VMEM((1, tq), F32),
                        pltpu.VMEM((dh + BF16_ROWS, tq), F32),
                        pltpu.VMEM((tk, tq), F32), pltpu.VMEM((tk, tq), F32),
                        pltpu.VMEM((2, 1, tq), F32)],
        compiler_params=_params("arbitrary", "arbitrary"),
        name="fox_attn",
    )(r, qt, k, gp, vt, *cast_along)


def _out_proj_kernel(yl_ref, ya_ref, x_ref, ga_ref, wl_ref, wa_ref, gp_ref, gt_ref,
                     g2_ref, sh_ref, sc_ref, x1_ref, h2_ref):
    ya = ya_ref[...].astype(F32)
    ya = (ya * _rms_scale(ya) * ga_ref[...]).astype(BF16)
    y = jnp.dot(yl_ref[...], wl_ref[...], preferred_element_type=F32)
    y = y + jnp.dot(ya, wa_ref[...], preferred_element_type=F32)
    x1 = x_ref[...] + (y * _rms_scale(y)) * (gt_ref[...] * gp_ref[...])
    x1_ref[...] = x1
    h2_ref[...] = ((x1 * _rms_scale(x1)) * (g2_ref[...] * (1.0 + sc_ref[...])) + sh_ref[...]).astype(BF16)


def _out_proj(yl, ya, x2d, g_att, w_o, g_post, gt1, g_pre2, sh2, sc2, tm=512):
    s, d = x2d.shape
    c = yl.shape[1]
    assert ya.shape[1] == c and w_o.shape == (2 * c, d)
    row = lambda i: (i, 0)
    vec_d = pl.BlockSpec((1, d), lambda i: (0, 0))
    vec_c = pl.BlockSpec((1, c), lambda i: (0, 0))
    half = lambda blk: pl.BlockSpec((c, d), lambda i: (blk, 0), pipeline_mode=pl.Buffered(1))
    return pl.pallas_call(
        _out_proj_kernel,
        grid=(s // tm,),
        in_specs=[pl.BlockSpec((tm, c), row), pl.BlockSpec((tm, c), row), pl.BlockSpec((tm, d), row),
                  vec_c, half(0), half(1), vec_d, vec_d,
                  vec_d, vec_d, vec_d],
        out_specs=[pl.BlockSpec((tm, d), row), pl.BlockSpec((tm, d), row)],
        out_shape=[jax.ShapeDtypeStruct((s, d), F32), jax.ShapeDtypeStruct((s, d), BF16)],
        compiler_params=_params("arbitrary"),
        name="out_proj",
    )(yl, ya, x2d, g_att, w_o, w_o, g_post, gt1, g_pre2, sh2, sc2)


def _ffn_kernel(h_ref, w1_ref, w2_ref, x_hbm, gp_ref, gt_ref, o_ref, xbuf_ref, xsem):
    i = pl.program_id(0)
    j = pl.program_id(1)
    tm = xbuf_ref.shape[0]

    def residual_copy():
        r0 = pl.multiple_of(i * tm, tm)
        return pltpu.make_async_copy(x_hbm.at[pl.ds(r0, tm), :], xbuf_ref, xsem)

    @pl.when(j == 0)
    def _():
        residual_copy().start()
        o_ref[...] = jnp.zeros_like(o_ref)

    hid = jnp.dot(h_ref[...], w1_ref[...], preferred_element_type=F32)
    hid = jnp.maximum(hid, 0.0)
    hid = (hid * hid).astype(BF16)
    o_ref[...] += jnp.dot(hid, w2_ref[...], preferred_element_type=F32)

    @pl.when(j == pl.num_programs(1) - 1)
    def _():
        residual_copy().wait()
        y = o_ref[...]
        o_ref[...] = xbuf_ref[...] + (y * _rms_scale(y)) * (gt_ref[...] * gp_ref[...])


def _ffn(h2, w1, w2, x1, g_post, gt2, tm=512, tf=2048):
    s, d = x1.shape
    dff = w1.shape[1]
    vec = pl.BlockSpec((1, d), lambda i, j: (0, 0))
    return pl.pallas_call(
        _ffn_kernel,
        grid=(s // tm, dff // tf),
        in_specs=[pl.BlockSpec((tm, d), lambda i, j: (i, 0)),
                  pl.BlockSpec((d, tf), lambda i, j: (0, j)),
                  pl.BlockSpec((tf, d), lambda i, j: (j, 0)),
                  pl.BlockSpec(memory_space=pl.ANY),
                  vec, vec],
        out_specs=pl.BlockSpec((tm, d), lambda i, j: (i, 0)),
        out_shape=jax.ShapeDtypeStruct((s, d), F32),
        scratch_shapes=[pltpu.VMEM((tm, d), F32), pltpu.SemaphoreType.DMA(())],
        compiler_params=_params("arbitrary", "arbitrary"),
        name="ffn",
    )(h2, w1, w2, x1, g_post, gt2)


def _block_diag(w):
    n, bs, _ = w.shape
    eye = jnp.eye(n, dtype=w.dtype)
    return jnp.einsum("nij,nm->nimj", w, eye).reshape(n * bs, n * bs)


def _gate_weights(w_rg, w_ig):
    n, bs, _ = w_rg.shape
    per = MXU_TILE // bs
    grouped = lambda w: jax.vmap(_block_diag)(w.reshape(n // per, per, bs, bs))
    return jnp.concatenate([grouped(w_rg), grouped(w_ig)], axis=-1).astype(BF16)


def kernel(x, c, w_ada, b_ada, g_pre_mix, g_post_mix, g_pre_mlp, g_post_mlp, w_in, conv_w, conv_b,
           w_rg, b_rg, w_ig, b_ig, lru_lambda, b_forget, g_lru_out, g_att_out, w_out, w_ff1, w_ff2):
    bsz, s, d = x.shape
    assert bsz == 1, "kernels are written for a single sequence"
    depth = w_ada.shape[0]
    d_lru = conv_w.shape[-1]
    n_heads = b_forget.shape[-1]
    d_att = n_heads * HEAD_DIM
    assert n_heads <= LANES

    x2d = x.reshape(s, d)
    c_col = c.reshape(d, 1)
    n_main = 2 * d_lru + 3 * d_att
    for l in range(depth):
        mod = _adaln(c_col, w_ada[l], b_ada[l][None, :])
        sh1, sc1, gt1, sh2, sc2, gt2 = jnp.split(mod, 6, axis=-1)

        w, wf = _cast_w_in(w_in[l], n_main)
        a, b, xg, qt, k, vt, f = _in_proj(
            x2d, g_pre_mix[l][None], sh1, sc1, w, wf, conv_w[l], conv_b[l][None],
            _gate_weights(w_rg[l], w_ig[l]), b_rg[l][None], b_ig[l][None], lru_lambda[l][None], d_att)

        b_pad = jnp.pad(b_forget[l], (0, LANES - n_heads))[None, :]
        gp, gabs = _fgate(f, b_pad, n_heads, ATT_TK)
        r = jnp.transpose(gabs[::ATT_TK, :n_heads])

        y_lru = _lru_scan(a, b, xg, g_lru_out[l][None])
        y_att, wo, w1, w2 = _fox_attention(r, qt, k, gp, vt, ATT_TQ, ATT_TK,
                                           cast_along=(w_out[l], w_ff1[l], w_ff2[l]))
        x2d, h2 = _out_proj(y_lru, y_att, x2d, g_att_out[l][None], wo,
                            g_post_mix[l][None], gt1, g_pre_mlp[l][None], sh2, sc2)
        x2d = _ffn(h2, w1, w2, x2d, g_post_mlp[l][None], gt2)
    return x2d.reshape(bsz, s, d)
```

```python
import functools
import math

import jax
import jax.numpy as jnp
from jax import lax
from jax.experimental import pallas as pl
from jax.experimental.pallas import tpu as pltpu

F32 = jnp.float32
BF16 = jnp.bfloat16

EPS = 1e-6
LRU_C = 8.0
CONV_WIDTH = 4
HEAD_DIM = 128
LOG2E = math.log2(math.e)
LANES = 128
SUBLANES = 8
BF16_ROWS = 16
MXU_TILE = 256
ATT_TK = 512
ATT_TQ = 2048
N_BIAS_PIECES = 3
VMEM_LIMIT_BYTES = 56 * 1024 * 1024


def _params(*sem):
    return pltpu.CompilerParams(dimension_semantics=sem, vmem_limit_bytes=VMEM_LIMIT_BYTES)


def _sigmoid(z):
    return 1.0 / (1.0 + jnp.exp(-z))


def _rms_scale(v):
    return lax.rsqrt(jnp.mean(v * v, axis=-1, keepdims=True) + EPS)


def _resident(shape):
    nd = len(shape)
    return pl.BlockSpec(shape, lambda *_: (0,) * nd, pipeline_mode=pl.Buffered(1))


def _adaln_kernel(c_ref, w_ref, b_ref, o_ref):
    c = c_ref[...]
    ca = c * _sigmoid(c)
    o_ref[...] = jnp.sum(ca * w_ref[...], axis=0, keepdims=True) + b_ref[...]


def _adaln(c_col, w_ada, b_ada, tn=1024):
    d, n = w_ada.shape
    return pl.pallas_call(
        _adaln_kernel,
        grid=(n // tn,),
        in_specs=[pl.BlockSpec((d, 1), lambda j: (0, 0)),
                  pl.BlockSpec((d, tn), lambda j: (0, j)),
                  pl.BlockSpec((1, tn), lambda j: (0, j))],
        out_specs=pl.BlockSpec((1, tn), lambda j: (0, j)),
        out_shape=jax.ShapeDtypeStruct((1, n), F32),
        compiler_params=_params("arbitrary"),
        name="adaln",
    )(c_col, w_ada, b_ada)


def _cast_w_in_kernel(w_ref, tail_ref, wm_ref, wf_ref):
    wm_ref[...] = w_ref[...].astype(BF16)
    n_tail, d = tail_ref.shape
    pad = jnp.zeros((wf_ref.shape[0] - n_tail, d), F32)
    wf_ref[...] = jnp.concatenate([tail_ref[...], pad], axis=0).astype(BF16)


def _cast_w_in(w_t, n_main, tr=512):
    n, d = w_t.shape
    n_tail = n - n_main
    assert n_tail % SUBLANES == 0 and n_tail <= LANES and n_main % n_tail == 0 and n_main % tr == 0
    return pl.pallas_call(
        _cast_w_in_kernel,
        grid=(n_main // tr,),
        in_specs=[pl.BlockSpec((tr, d), lambda i: (i, 0)),
                  pl.BlockSpec((n_tail, d), lambda i: (n_main // n_tail, 0))],
        out_specs=[pl.BlockSpec((tr, d), lambda i: (i, 0)), pl.BlockSpec((LANES, d), lambda i: (0, 0))],
        out_shape=[jax.ShapeDtypeStruct((n_main, d), BF16), jax.ShapeDtypeStruct((LANES, d), BF16)],
        compiler_params=_params("arbitrary"),
        name="cast_w_in",
    )(w_t, w_t)


def _dot_t(a, b_t):
    return lax.dot_general(a, b_t, (((1,), (1,)), ((), ())), preferred_element_type=F32)


def _in_proj_kernel(x_ref, g_ref, sh_ref, sc_ref, wu_ref, wq_ref, wk_ref, wv_ref, wf_ref,
                    cw_ref, cb_ref, wg_ref, brg_ref, big_ref, lam_ref,
                    a_ref, b_ref, xg_ref, qt_ref, k_ref, vt_ref, f_ref, ext_ref, *, n_heads, q_scale):
    tm = x_ref.shape[0]
    c = a_ref.shape[1]
    pad = SUBLANES

    @pl.when(pl.program_id(0) == 0)
    def _():
        ext_ref[0:pad, :] = jnp.zeros((pad, c), F32)

    x = x_ref[...]
    h = ((x * _rms_scale(x)) * (g_ref[...] * (1.0 + sc_ref[...])) + sh_ref[...]).astype(BF16)
    u = _dot_t(h, wu_ref[...])
    ug = u[:, c:]
    gelu = 0.5 * ug * (1.0 + jnp.tanh(math.sqrt(2.0 / math.pi) * (ug + 0.044715 * (ug * ug * ug))))
    xg_ref[...] = gelu.astype(BF16)

    ul = u[:, :c]
    ext_ref[pad:pad + tm, :] = ul
    xc = ul * cw_ref[CONV_WIDTH - 1:CONV_WIDTH, :] + cb_ref[...]
    for kk in range(CONV_WIDTH - 1):
        back = CONV_WIDTH - 1 - kk
        xc = xc + ext_ref[pad - back:pad - back + tm, :] * cw_ref[kk:kk + 1, :]
    ext_ref[0:pad, :] = ul[tm - pad:tm, :]

    gw = wg_ref.shape[1]
    gr, gi = [], []
    for grp in range(c // gw):
        gg = jnp.dot(xc[:, grp * gw:(grp + 1) * gw].astype(BF16), wg_ref[grp], preferred_element_type=F32)
        gr.append(gg[:, :gw])
        gi.append(gg[:, gw:])
    r = 0.5 + 0.5 * jnp.tanh(0.5 * (jnp.concatenate(gr, axis=1) + brg_ref[...]))
    ig = 0.5 + 0.5 * jnp.tanh(0.5 * (jnp.concatenate(gi, axis=1) + big_ref[...]))
    lam = lam_ref[...]
    sp = jnp.maximum(-lam, 0.0) + jnp.log(1.0 + jnp.exp(-jnp.abs(lam)))
    a = jnp.exp((-LRU_C) * r * sp)
    t = 1.0 - a * a
    mult = jnp.where(t > 0.0, t * lax.rsqrt(t), 0.0)
    a_ref[...] = a
    b_ref[...] = mult * (ig * xc)

    q = _dot_t(h, wq_ref[...]) * q_scale
    k = _dot_t(h, wk_ref[...])
    v = _dot_t(h, wv_ref[...])
    for hd in range(n_heads):
        sl = slice(hd * HEAD_DIM, (hd + 1) * HEAD_DIM)
        qt_ref[hd] = jnp.transpose(q[:, sl]).astype(BF16)
        k_ref[hd] = k[:, sl].astype(BF16)
        vt_ref[hd] = jnp.transpose(v[:, sl]).astype(BF16)
    f_ref[...] = _dot_t(h, wf_ref[...])


def _in_proj(x2d, g, sh, sc, w, wf, conv_w, conv_b, w_gates, b_rg, b_ig, lam, d_att, tm=512):
    s, d = x2d.shape
    c = conv_w.shape[1]
    n_u = 2 * c
    n_heads = d_att // HEAD_DIM
    assert n_u % d_att == 0
    q_blk = n_u // d_att

    def cols(width, blk):
        return pl.BlockSpec((width, d), lambda i: (blk, 0), pipeline_mode=pl.Buffered(1))

    row = lambda i: (i, 0)
    vec = pl.BlockSpec((1, d), lambda i: (0, 0))
    kern = functools.partial(_in_proj_kernel, n_heads=n_heads, q_scale=HEAD_DIM ** -0.5 * LOG2E)
    k_shape = jax.ShapeDtypeStruct((n_heads, s, HEAD_DIM), BF16)
    k_spec = pl.BlockSpec((n_heads, tm, HEAD_DIM), lambda i: (0, i, 0))
    t_shape = jax.ShapeDtypeStruct((n_heads, HEAD_DIM, s), BF16)
    t_spec = pl.BlockSpec((n_heads, HEAD_DIM, tm), lambda i: (0, 0, i))
    vec_c = pl.BlockSpec((1, c), lambda i: (0, 0))
    lru_spec = pl.BlockSpec((tm, c), row)
    return pl.pallas_call(
        kern,
        grid=(s // tm,),
        in_specs=[pl.BlockSpec((tm, d), row), vec, vec, vec,
                  cols(n_u, 0), cols(d_att, q_blk), cols(d_att, q_blk + 1), cols(d_att, q_blk + 2),
                  _resident(wf.shape),
                  pl.BlockSpec((CONV_WIDTH, c), lambda i: (0, 0)), vec_c, _resident(w_gates.shape),
                  vec_c, vec_c, vec_c],
        out_specs=[lru_spec, lru_spec, lru_spec, t_spec, k_spec, t_spec,
                   pl.BlockSpec((tm, LANES), row)],
        out_shape=[jax.ShapeDtypeStruct((s, c), F32), jax.ShapeDtypeStruct((s, c), F32),
                   jax.ShapeDtypeStruct((s, c), BF16), t_shape, k_shape, t_shape,
                   jax.ShapeDtypeStruct((s, LANES), F32)],
        scratch_shapes=[pltpu.VMEM((tm + SUBLANES, c), F32)],
        compiler_params=_params("arbitrary"),
        name="in_proj",
    )(x2d, g, sh, sc, w, w, w, w, wf, conv_w, conv_b, w_gates, b_rg, b_ig, lam)


def _fgate_kernel(f_ref, b_ref, gp_ref, gabs_ref, carry_ref, *, tm, tk, n_heads):
    @pl.when(pl.program_id(0) == 0)
    def _():
        carry_ref[...] = jnp.zeros_like(carry_ref)

    z = f_ref[...] + b_ref[...]
    inc = (jnp.log(1.0 + jnp.exp(-jnp.abs(z))) - jnp.minimum(z, 0.0)) * LOG2E
    row = lax.broadcasted_iota(jnp.int32, (tm, LANES), 0)
    acc = inc
    d = 1
    while d < tm:
        acc = acc + jnp.where(row >= d, pltpu.roll(acc, d, axis=0), 0.0)
        d *= 2
    carry = carry_ref[0:1, :]
    gabs_ref[...] = acc + carry
    carry_ref[...] = jnp.broadcast_to(acc[tm - 1:tm, :] + carry, carry_ref.shape)

    lane = lax.broadcasted_iota(jnp.int32, (tk, LANES), 1)
    for blk in range(tm // tk):
        a = acc[blk * tk:(blk + 1) * tk, :]
        rel = a - a[0:1, :]
        for hd in range(n_heads):
            rem = jnp.broadcast_to(rel[:, hd:hd + 1], (tk, LANES))
            out = jnp.zeros((tk, LANES), F32)
            for piece in range(N_BIAS_PIECES):
                pc = rem.astype(BF16).astype(F32)
                out = jnp.where(lane == piece, pc, out)
                rem = rem - pc
            gp_ref[hd, blk * tk:(blk + 1) * tk, :] = out.astype(BF16)


def _fgate(f, b_pad, n_heads, tk, tm=1024):
    s = f.shape[0]
    return pl.pallas_call(
        functools.partial(_fgate_kernel, tm=tm, tk=tk, n_heads=n_heads),
        grid=(s // tm,),
        in_specs=[pl.BlockSpec((tm, LANES), lambda i: (i, 0)),
                  pl.BlockSpec((1, LANES), lambda i: (0, 0))],
        out_specs=[pl.BlockSpec((n_heads, tm, LANES), lambda i: (0, i, 0)),
                   pl.BlockSpec((tm, LANES), lambda i: (i, 0))],
        out_shape=[jax.ShapeDtypeStruct((n_heads, s, LANES), BF16),
                   jax.ShapeDtypeStruct((s, LANES), F32)],
        scratch_shapes=[pltpu.VMEM((SUBLANES, LANES), F32)],
        compiler_params=_params("arbitrary"),
        name="fgate",
    )(f, b_pad)


def _lru_scan_kernel(a_ref, b_ref, xg_ref, go_ref, y_ref, h_ref, hc_ref, *, tt, c):
    @pl.when(pl.program_id(0) == 0)
    def _():
        hc_ref[...] = jnp.zeros_like(hc_ref)

    row = lax.broadcasted_iota(jnp.int32, (SUBLANES, c), 0)

    def group(gi, h_prev):
        r0 = pl.multiple_of(gi * SUBLANES, SUBLANES)
        av = a_ref[pl.ds(r0, SUBLANES), :]
        bv = b_ref[pl.ds(r0, SUBLANES), :]
        d = 1
        while d < SUBLANES:
            a_sh = jnp.where(row >= d, pltpu.roll(av, d, axis=0), 1.0)
            b_sh = jnp.where(row >= d, pltpu.roll(bv, d, axis=0), 0.0)
            bv = bv + av * b_sh
            av = av * a_sh
            d *= 2
        hv = bv + av * h_prev
        h_ref[pl.ds(r0, SUBLANES), :] = hv
        return jnp.broadcast_to(hv[SUBLANES - 1:SUBLANES, :], (SUBLANES, c))

    hc_ref[...] = lax.fori_loop(0, tt // SUBLANES, group, hc_ref[...], unroll=4)

    y = h_ref[...] * xg_ref[...].astype(F32)
    y_ref[...] = (y * _rms_scale(y) * go_ref[...]).astype(BF16)


def _lru_scan(a, b, xg, g_out, tt=1024):
    s, c = a.shape
    blk = pl.BlockSpec((tt, c), lambda i: (i, 0))
    return pl.pallas_call(
        functools.partial(_lru_scan_kernel, tt=tt, c=c),
        grid=(s // tt,),
        in_specs=[blk, blk, blk, pl.BlockSpec((1, c), lambda i: (0, 0))],
        out_specs=blk,
        out_shape=jax.ShapeDtypeStruct((s, c), BF16),
        scratch_shapes=[pltpu.VMEM((tt, c), F32),
                        pltpu.VMEM((SUBLANES, c), F32)],
        compiler_params=_params("arbitrary"),
        name="lru_scan",
    )(a, b, xg, g_out)


def _fox_kernel(r_ref, qt_ref, k_ref, g_ref, vt_ref, *rest, tq, tk, n_cast):
    cast_in, (o_ref, *cast_out) = rest[:n_cast], rest[n_cast:2 * n_cast + 1]
    m_ref, acc_ref, s0_ref, s1_ref, cm_ref = rest[2 * n_cast + 1:]
    for w_in_ref, w_out_ref in zip(cast_in, cast_out):
        w_out_ref[...] = w_in_ref[...].astype(BF16)

    kpq = tq // tk
    assert tq == kpq * tk and kpq % 2 == 0
    hd = pl.program_id(0)
    qi = pl.program_id(1)
    dh = qt_ref.shape[1]
    n_full = kpq * qi
    row = lax.broadcasted_iota(jnp.int32, (LANES, tq), 0)
    ones_q = jnp.where(row < N_BIAS_PIECES, 1.0, 0.0).astype(BF16)
    qt_aug = jnp.concatenate([qt_ref[0], ones_q], axis=0)
    ones_v = jnp.ones((BF16_ROWS, tk), BF16)
    r_q = r_ref[hd, n_full]
    m_ref[...] = jnp.full_like(m_ref, -jnp.inf)
    acc_ref[...] = jnp.zeros_like(acc_ref)
    s_refs = (s0_ref, s1_ref)

    def scores(kb, diag_j, slot):
        c0 = 0 if diag_j is None else diag_j * tk
        k0 = pl.multiple_of(kb * tk, tk)
        k_aug = jnp.concatenate([k_ref[0, pl.ds(k0, tk), :], g_ref[0, pl.ds(k0, tk), :]], axis=1)
        s = jnp.dot(k_aug, qt_aug[:, c0:], preferred_element_type=F32)
        if diag_j is not None:
            kk = lax.broadcasted_iota(jnp.int32, s.shape, 0)
            qq = lax.broadcasted_iota(jnp.int32, s.shape, 1)
            s = jnp.where(kk <= qq, s, -jnp.inf)
        s_refs[slot][:, c0:] = s
        cm_ref[slot, :, c0:] = jnp.max(s, axis=0, keepdims=True)

    def update(kb, diag_j, slot):
        c0 = 0 if diag_j is None else diag_j * tk
        k0 = pl.multiple_of(kb * tk, tk)
        shift = r_ref[hd, kb] - r_q
        m_prev = m_ref[:, c0:]
        m_new = jnp.maximum(m_prev, cm_ref[slot, :, c0:] + shift)
        alpha = jnp.exp2(m_prev - m_new)
        p = jnp.exp2(s_refs[slot][:, c0:] - (m_new - shift)).astype(BF16)
        vt_aug = jnp.concatenate([vt_ref[0, :, pl.ds(k0, tk)], ones_v], axis=0)
        acc_ref[:, c0:] = alpha * acc_ref[:, c0:] + jnp.dot(vt_aug, p, preferred_element_type=F32)
        m_ref[:, c0:] = m_new

    pre = [kpq - 1] + [j for j in range(1, kpq - 1) if j != kpq // 2] + [0]
    post = [kpq // 2] if kpq > 2 else []
    n_pre = len(pre)
    scores(n_full + pre[0], pre[0], 0)
    for p in range(1, n_pre):
        scores(n_full + pre[p], pre[p], p % 2)
        update(n_full + pre[p - 1], pre[p - 1], (p - 1) % 2)

    def body(u, carry):
        for t in range(kpq):
            blk = kpq * u + t
            scores(blk, None, (n_pre + t) % 2)
            prev = jnp.where(u == 0, n_full, blk - 1) if t == 0 else blk - 1
            update(prev, None, (n_pre + t - 1) % 2)
        return carry

    lax.fori_loop(0, qi, body, 0)
    pending = (jnp.where(qi == 0, n_full, n_full - 1), None, (n_pre - 1) % 2)
    for i, j in enumerate(post):
        scores(n_full + j, j, (n_pre + i) % 2)
        update(*pending)
        pending = (n_full + j, j, (n_pre + i) % 2)
    update(*pending)
    out_t = acc_ref[0:dh, :] / acc_ref[dh:dh + 1, :]
    o_ref[...] = jnp.transpose(out_t).astype(o_ref.dtype)


def _fox_attention(r, qt, k, gp, vt, tq, tk, cast_along=()):
    n_heads, s, dh = k.shape
    nq = s // tq
    kern = functools.partial(_fox_kernel, tq=tq, tk=tk, n_cast=len(cast_along))
    head = lambda h, i: (h, 0, 0)
    cast_specs = []
    for wgt in cast_along:
        rows = wgt.shape[0] // (n_heads * nq)
        assert rows * n_heads * nq == wgt.shape[0] and rows % BF16_ROWS == 0
        cast_specs.append(pl.BlockSpec((rows, wgt.shape[1]), lambda h, i: (h * nq + i, 0)))
    return pl.pallas_call(
        kern,
        grid=(n_heads, nq),
        in_specs=[pl.BlockSpec(memory_space=pltpu.SMEM),
                  pl.BlockSpec((1, dh, tq), lambda h, i: (h, 0, i)),
                  pl.BlockSpec((1, s, dh), head),
                  pl.BlockSpec((1, s, LANES), head),
                  pl.BlockSpec((1, dh, s), head)] + cast_specs,
        out_specs=[pl.BlockSpec((tq, dh), lambda h, i: (i, h))] + cast_specs,
        out_shape=[jax.ShapeDtypeStruct((s, n_heads * dh), BF16)]
        + [jax.ShapeDtypeStruct(wgt.shape, BF16) for wgt in cast_along],
        scratch_shapes=[pltpu.VMEM((1, tq), F32),
                        pltpu.VMEM((dh + BF16_ROWS, tq), F32),
                        pltpu.VMEM((tk, tq), F32), pltpu.VMEM((tk, tq), F32),
                        pltpu.VMEM((2, 1, tq), F32)],
        compiler_params=_params("arbitrary", "arbitrary"),
        name="fox_attn",
    )(r, qt, k, gp, vt, *cast_along)


def _out_proj_kernel(yl_ref, ya_ref, x_ref, ga_ref, wl_ref, wa_ref, gp_ref, gt_ref,
                     g2_ref, sh_ref, sc_ref, x1_ref, h2_ref):
    ya = ya_ref[...].astype(F32)
    ya = (ya * _rms_scale(ya) * ga_ref[...]).astype(BF16)
    y = jnp.dot(yl_ref[...], wl_ref[...], preferred_element_type=F32)
    y = y + jnp.dot(ya, wa_ref[...], preferred_element_type=F32)
    x1 = x_ref[...] + (y * _rms_scale(y)) * (gt_ref[...] * gp_ref[...])
    x1_ref[...] = x1
    h2_ref[...] = ((x1 * _rms_scale(x1)) * (g2_ref[...] * (1.0 + sc_ref[...])) + sh_ref[...]).astype(BF16)


def _out_proj(yl, ya, x2d, g_att, w_o, g_post, gt1, g_pre2, sh2, sc2, tm=512):
    s, d = x2d.shape
    c = yl.shape[1]
    assert ya.shape[1] == c and w_o.shape == (2 * c, d)
    row = lambda i: (i, 0)
    vec_d = pl.BlockSpec((1, d), lambda i: (0, 0))
    vec_c = pl.BlockSpec((1, c), lambda i: (0, 0))
    half = lambda blk: pl.BlockSpec((c, d), lambda i: (blk, 0), pipeline_mode=pl.Buffered(1))
    return pl.pallas_call(
        _out_proj_kernel,
        grid=(s // tm,),
        in_specs=[pl.BlockSpec((tm, c), row), pl.BlockSpec((tm, c), row), pl.BlockSpec((tm, d), row),
                  vec_c, half(0), half(1), vec_d, vec_d,
                  vec_d, vec_d, vec_d],
        out_specs=[pl.BlockSpec((tm, d), row), pl.BlockSpec((tm, d), row)],
        out_shape=[jax.ShapeDtypeStruct((s, d), F32), jax.ShapeDtypeStruct((s, d), BF16)],
        compiler_params=_params("arbitrary"),
        name="out_proj",
    )(yl, ya, x2d, g_att, w_o, w_o, g_post, gt1, g_pre2, sh2, sc2)


def _ffn_kernel(h_ref, w1_ref, w2_ref, x_hbm, gp_ref, gt_ref, o_ref, xbuf_ref, xsem):
    i = pl.program_id(0)
    j = pl.program_id(1)
    tm = xbuf_ref.shape[0]

    def residual_copy():
        r0 = pl.multiple_of(i * tm, tm)
        return pltpu.make_async_copy(x_hbm.at[pl.ds(r0, tm), :], xbuf_ref, xsem)

    @pl.when(j == 0)
    def _():
        residual_copy().start()
        o_ref[...] = jnp.zeros_like(o_ref)

    hid = jnp.dot(h_ref[...], w1_ref[...], preferred_element_type=F32)
    hid = jnp.maximum(hid, 0.0)
    hid = (hid * hid).astype(BF16)
    o_ref[...] += jnp.dot(hid, w2_ref[...], preferred_element_type=F32)

    @pl.when(j == pl.num_programs(1) - 1)
    def _():
        residual_copy().wait()
        y = o_ref[...]
        o_ref[...] = xbuf_ref[...] + (y * _rms_scale(y)) * (gt_ref[...] * gp_ref[...])


def _ffn(h2, w1, w2, x1, g_post, gt2, tm=512, tf=2048):
    s, d = x1.shape
    dff = w1.shape[1]
    vec = pl.BlockSpec((1, d), lambda i, j: (0, 0))
    return pl.pallas_call(
        _ffn_kernel,
        grid=(s // tm, dff // tf),
        in_specs=[pl.BlockSpec((tm, d), lambda i, j: (i, 0)),
                  pl.BlockSpec((d, tf), lambda i, j: (0, j)),
                  pl.BlockSpec((tf, d), lambda i, j: (j, 0)),
                  pl.BlockSpec(memory_space=pl.ANY),
                  vec, vec],
        out_specs=pl.BlockSpec((tm, d), lambda i, j: (i, 0)),
        out_shape=jax.ShapeDtypeStruct((s, d), F32),
        scratch_shapes=[pltpu.VMEM((tm, d), F32), pltpu.SemaphoreType.DMA(())],
        compiler_params=_params("arbitrary", "arbitrary"),
        name="ffn",
    )(h2, w1, w2, x1, g_post, gt2)


def _block_diag(w):
    n, bs, _ = w.shape
    eye = jnp.eye(n, dtype=w.dtype)
    return jnp.einsum("nij,nm->nimj", w, eye).reshape(n * bs, n * bs)


def _gate_weights(w_rg, w_ig):
    n, bs, _ = w_rg.shape
    per = MXU_TILE // bs
    grouped = lambda w: jax.vmap(_block_diag)(w.reshape(n // per, per, bs, bs))
    return jnp.concatenate([grouped(w_rg), grouped(w_ig)], axis=-1).astype(BF16)


def kernel(x, c, w_ada, b_ada, g_pre_mix, g_post_mix, g_pre_mlp, g_post_mlp, w_in, conv_w, conv_b,
           w_rg, b_rg, w_ig, b_ig, lru_lambda, b_forget, g_lru_out, g_att_out, w_out, w_ff1, w_ff2):
    bsz, s, d = x.shape
    assert bsz == 1, "kernels are written for a single sequence"
    depth = w_ada.shape[0]
    d_lru = conv_w.shape[-1]
    n_heads = b_forget.shape[-1]
    d_att = n_heads * HEAD_DIM
    assert n_heads <= LANES

    x2d = x.reshape(s, d)
    c_col = c.reshape(d, 1)
    n_main = 2 * d_lru + 3 * d_att
    for l in range(depth):
        mod = _adaln(c_col, w_ada[l], b_ada[l][None, :])
        sh1, sc1, gt1, sh2, sc2, gt2 = jnp.split(mod, 6, axis=-1)

        w, wf = _cast_w_in(jnp.swapaxes(w_in[l], 0, 1), n_main)
        a, b, xg, qt, k, vt, f = _in_proj(
            x2d, g_pre_mix[l][None], sh1, sc1, w, wf, conv_w[l], conv_b[l][None],
            _gate_weights(w_rg[l], w_ig[l]), b_rg[l][None], b_ig[l][None], lru_lambda[l][None], d_att)

        b_pad = jnp.pad(b_forget[l], (0, LANES - n_heads))[None, :]
        gp, gabs = _fgate(f, b_pad, n_heads, ATT_TK)
        r = jnp.transpose(gabs[::ATT_TK, :n_heads])

        y_lru = _lru_scan(a, b, xg, g_lru_out[l][None])
        y_att, wo, w1, w2 = _fox_attention(r, qt, k, gp, vt, ATT_TQ, ATT_TK,
                                           cast_along=(w_out[l], w_ff1[l], w_ff2[l]))
        x2d, h2 = _out_proj(y_lru, y_att, x2d, g_att_out[l][None], wo,
                            g_post_mix[l][None], gt1, g_pre_mlp[l][None], sh2, sc2)
        x2d = _ffn(h2, w1, w2, x2d, g_post_mlp[l][None], gt2)
    return x2d.reshape(bsz, s, d)
```

```python
import functools
import math

import jax
import jax.numpy as jnp
from jax import lax
from jax.experimental import pallas as pl
from jax.experimental.pallas import tpu as pltpu

F32 = jnp.float32
BF16 = jnp.bfloat16

EPS = 1e-6
LRU_C = 8.0
CONV_WIDTH = 4
HEAD_DIM = 128
LOG2E = math.log2(math.e)
LANES = 128
SUBLANES = 8
BF16_ROWS = 16
MXU_TILE = 256
ATT_TK = 512
ATT_TQ = 2048
N_BIAS_PIECES = 3
VMEM_LIMIT_BYTES = 56 * 1024 * 1024


def _params(*sem):
    return pltpu.CompilerParams(dimension_semantics=sem, vmem_limit_bytes=VMEM_LIMIT_BYTES)


def _sigmoid(z):
    return 1.0 / (1.0 + jnp.exp(-z))


def _rms_scale(v):
    return lax.rsqrt(jnp.mean(v * v, axis=-1, keepdims=True) + EPS)


def _resident(shape):
    nd = len(shape)
    return pl.BlockSpec(shape, lambda *_: (0,) * nd, pipeline_mode=pl.Buffered(1))


def _adaln_kernel(c_ref, w_ref, b_ref, o_ref):
    c = c_ref[...]
    ca = c * _sigmoid(c)
    o_ref[...] = jnp.sum(ca * w_ref[...], axis=0, keepdims=True) + b_ref[...]


def _adaln(c_col, w_ada, b_ada, tn=1024):
    d, n = w_ada.shape
    return pl.pallas_call(
        _adaln_kernel,
        grid=(n // tn,),
        in_specs=[pl.BlockSpec((d, 1), lambda j: (0, 0)),
                  pl.BlockSpec((d, tn), lambda j: (0, j)),
                  pl.BlockSpec((1, tn), lambda j: (0, j))],
        out_specs=pl.BlockSpec((1, tn), lambda j: (0, j)),
        out_shape=jax.ShapeDtypeStruct((1, n), F32),
        compiler_params=_params("arbitrary"),
        name="adaln",
    )(c_col, w_ada, b_ada)


def _cast_w_in_kernel(w_ref, tail_ref, wm_ref, wf_ref):
    wm_ref[...] = w_ref[...].astype(BF16)
    n_tail, d = tail_ref.shape
    pad = jnp.zeros((wf_ref.shape[0] - n_tail, d), F32)
    wf_ref[...] = jnp.concatenate([tail_ref[...], pad], axis=0).astype(BF16)


def _cast_w_in(w_t, n_main, tr=512):
    n, d = w_t.shape
    n_tail = n - n_main
    assert n_tail % SUBLANES == 0 and n_tail <= LANES and n_main % n_tail == 0 and n_main % tr == 0
    return pl.pallas_call(
        _cast_w_in_kernel,
        grid=(n_main // tr,),
        in_specs=[pl.BlockSpec((tr, d), lambda i: (i, 0)),
                  pl.BlockSpec((n_tail, d), lambda i: (n_main // n_tail, 0))],
        out_specs=[pl.BlockSpec((tr, d), lambda i: (i, 0)), pl.BlockSpec((LANES, d), lambda i: (0, 0))],
        out_shape=[jax.ShapeDtypeStruct((n_main, d), BF16), jax.ShapeDtypeStruct((LANES, d), BF16)],
        compiler_params=_params("arbitrary"),
        name="cast_w_in",
    )(w_t, w_t)


def _dot_t(a, b_t):
    return lax.dot_general(a, b_t, (((1,), (1,)), ((), ())), preferred_element_type=F32)


def _in_proj_kernel(x_ref, g_ref, sh_ref, sc_ref, wu_ref, wq_ref, wk_ref, wv_ref, wf_ref,
                    cw_ref, cb_ref, wg_ref, brg_ref, big_ref, lam_ref,
                    a_ref, b_ref, xg_ref, qt_ref, k_ref, vt_ref, f_ref, ext_ref, *, n_heads, q_scale):
    tm = x_ref.shape[0]
    c = a_ref.shape[1]
    pad = SUBLANES

    @pl.when(pl.program_id(0) == 0)
    def _():
        ext_ref[0:pad, :] = jnp.zeros((pad, c), F32)

    x = x_ref[...]
    h = ((x * _rms_scale(x)) * (g_ref[...] * (1.0 + sc_ref[...])) + sh_ref[...]).astype(BF16)
    u = _dot_t(h, wu_ref[...])
    ug = u[:, c:]
    gelu = 0.5 * ug * (1.0 + jnp.tanh(math.sqrt(2.0 / math.pi) * (ug + 0.044715 * (ug * ug * ug))))
    xg_ref[...] = gelu.astype(BF16)

    ul = u[:, :c]
    ext_ref[pad:pad + tm, :] = ul
    xc = ul * cw_ref[CONV_WIDTH - 1:CONV_WIDTH, :] + cb_ref[...]
    for kk in range(CONV_WIDTH - 1):
        back = CONV_WIDTH - 1 - kk
        xc = xc + ext_ref[pad - back:pad - back + tm, :] * cw_ref[kk:kk + 1, :]
    ext_ref[0:pad, :] = ul[tm - pad:tm, :]

    gw = wg_ref.shape[1]
    gr, gi = [], []
    for grp in range(c // gw):
        gg = jnp.dot(xc[:, grp * gw:(grp + 1) * gw].astype(BF16), wg_ref[grp], preferred_element_type=F32)
        gr.append(gg[:, :gw])
        gi.append(gg[:, gw:])
    r = 0.5 + 0.5 * jnp.tanh(0.5 * (jnp.concatenate(gr, axis=1) + brg_ref[...]))
    ig = 0.5 + 0.5 * jnp.tanh(0.5 * (jnp.concatenate(gi, axis=1) + big_ref[...]))
    lam = lam_ref[...]
    sp = jnp.maximum(-lam, 0.0) + jnp.log(1.0 + jnp.exp(-jnp.abs(lam)))
    a = jnp.exp((-LRU_C) * r * sp)
    t = 1.0 - a * a
    mult = jnp.where(t > 0.0, t * lax.rsqrt(t), 0.0)
    a_ref[...] = a
    b_ref[...] = mult * (ig * xc)

    q = _dot_t(h, wq_ref[...]) * q_scale
    k = _dot_t(h, wk_ref[...])
    v = _dot_t(h, wv_ref[...])
    for hd in range(n_heads):
        sl = slice(hd * HEAD_DIM, (hd + 1) * HEAD_DIM)
        qt_ref[hd] = jnp.transpose(q[:, sl]).astype(BF16)
        k_ref[hd] = k[:, sl].astype(BF16)
        vt_ref[hd] = jnp.transpose(v[:, sl]).astype(BF16)
    f_ref[...] = _dot_t(h, wf_ref[...])


def _in_proj(x2d, g, sh, sc, w, wf, conv_w, conv_b, w_gates, b_rg, b_ig, lam, d_att, tm=512):
    s, d = x2d.shape
    c = conv_w.shape[1]
    n_u = 2 * c
    n_heads = d_att // HEAD_DIM
    assert n_u % d_att == 0
    q_blk = n_u // d_att

    def cols(width, blk):
        return pl.BlockSpec((width, d), lambda i: (blk, 0), pipeline_mode=pl.Buffered(1))

    row = lambda i: (i, 0)
    vec = pl.BlockSpec((1, d), lambda i: (0, 0))
    kern = functools.partial(_in_proj_kernel, n_heads=n_heads, q_scale=HEAD_DIM ** -0.5 * LOG2E)
    k_shape = jax.ShapeDtypeStruct((n_heads, s, HEAD_DIM), BF16)
    k_spec = pl.BlockSpec((n_heads, tm, HEAD_DIM), lambda i: (0, i, 0))
    t_shape = jax.ShapeDtypeStruct((n_heads, HEAD_DIM, s), BF16)
    t_spec = pl.BlockSpec((n_heads, HEAD_DIM, tm), lambda i: (0, 0, i))
    vec_c = pl.BlockSpec((1, c), lambda i: (0, 0))
    lru_spec = pl.BlockSpec((tm, c), row)
    return pl.pallas_call(
        kern,
        grid=(s // tm,),
        in_specs=[pl.BlockSpec((tm, d), row), vec, vec, vec,
                  cols(n_u, 0), cols(d_att, q_blk), cols(d_att, q_blk + 1), cols(d_att, q_blk + 2),
                  _resident(wf.shape),
                  pl.BlockSpec((CONV_WIDTH, c), lambda i: (0, 0)), vec_c, _resident(w_gates.shape),
                  vec_c, vec_c, vec_c],
        out_specs=[lru_spec, lru_spec, lru_spec, t_spec, k_spec, t_spec,
                   pl.BlockSpec((tm, LANES), row)],
        out_shape=[jax.ShapeDtypeStruct((s, c), F32), jax.ShapeDtypeStruct((s, c), F32),
                   jax.ShapeDtypeStruct((s, c), BF16), t_shape, k_shape, t_shape,
                   jax.ShapeDtypeStruct((s, LANES), F32)],
        scratch_shapes=[pltpu.VMEM((tm + SUBLANES, c), F32)],
        compiler_params=_params("arbitrary"),
        name="in_proj",
    )(x2d, g, sh, sc, w, w, w, w, wf, conv_w, conv_b, w_gates, b_rg, b_ig, lam)


def _fgate_kernel(f_ref, b_ref, sel_ref, gp_ref, gabs_ref, carry_ref, *, tm, tk, n_heads):
    @pl.when(pl.program_id(0) == 0)
    def _():
        carry_ref[...] = jnp.zeros_like(carry_ref)

    z = f_ref[...] + b_ref[...]
    inc = (jnp.log(1.0 + jnp.exp(-jnp.abs(z))) - jnp.minimum(z, 0.0)) * LOG2E
    row = lax.broadcasted_iota(jnp.int32, (tm, LANES), 0)
    acc = inc
    d = 1
    while d < tm:
        acc = acc + jnp.where(row >= d, pltpu.roll(acc, d, axis=0), 0.0)
        d *= 2
    carry = carry_ref[0:1, :]
    gabs_ref[...] = acc + carry
    carry_ref[...] = jnp.broadcast_to(acc[tm - 1:tm, :] + carry, carry_ref.shape)

    for blk in range(tm // tk):
        a = acc[blk * tk:(blk + 1) * tk, :]
        rem = a - a[0:1, :]
        pieces = []
        for _ in range(N_BIAS_PIECES):
            pc = rem.astype(BF16)
            pieces.append(pc)
            rem = rem - pc.astype(F32)
        spread = jnp.dot(jnp.concatenate(pieces, axis=1), sel_ref[...], preferred_element_type=F32)
        for hd in range(n_heads):
            gp_ref[hd, blk * tk:(blk + 1) * tk, :] = spread[:, hd * LANES:(hd + 1) * LANES].astype(BF16)


def _fgate(f, b_pad, n_heads, tk, tm=1024):
    s = f.shape[0]
    src = jnp.arange(N_BIAS_PIECES * LANES)[:, None]
    dst = jnp.arange(n_heads * LANES)[None, :]
    sel = ((src // LANES == dst % LANES) & (src % LANES == dst // LANES)).astype(BF16)
    return pl.pallas_call(
        functools.partial(_fgate_kernel, tm=tm, tk=tk, n_heads=n_heads),
        grid=(s // tm,),
        in_specs=[pl.BlockSpec((tm, LANES), lambda i: (i, 0)),
                  pl.BlockSpec((1, LANES), lambda i: (0, 0)),
                  _resident(sel.shape)],
        out_specs=[pl.BlockSpec((n_heads, tm, LANES), lambda i: (0, i, 0)),
                   pl.BlockSpec((tm, LANES), lambda i: (i, 0))],
        out_shape=[jax.ShapeDtypeStruct((n_heads, s, LANES), BF16),
                   jax.ShapeDtypeStruct((s, LANES), F32)],
        scratch_shapes=[pltpu.VMEM((SUBLANES, LANES), F32)],
        compiler_params=_params("arbitrary"),
        name="fgate",
    )(f, b_pad, sel)


def _lru_scan_kernel(a_ref, b_ref, xg_ref, go_ref, y_ref, h_ref, hc_ref, *, tt, c):
    @pl.when(pl.program_id(0) == 0)
    def _():
        hc_ref[...] = jnp.zeros_like(hc_ref)

    row = lax.broadcasted_iota(jnp.int32, (SUBLANES, c), 0)

    def group(gi, h_prev):
        r0 = pl.multiple_of(gi * SUBLANES, SUBLANES)
        av = a_ref[pl.ds(r0, SUBLANES), :]
        bv = b_ref[pl.ds(r0, SUBLANES), :]
        d = 1
        while d < SUBLANES:
            a_sh = jnp.where(row >= d, pltpu.roll(av, d, axis=0), 1.0)
            b_sh = jnp.where(row >= d, pltpu.roll(bv, d, axis=0), 0.0)
            bv = bv + av * b_sh
            av = av * a_sh
            d *= 2
        hv = bv + av * h_prev
        h_ref[pl.ds(r0, SUBLANES), :] = hv
        return jnp.broadcast_to(hv[SUBLANES - 1:SUBLANES, :], (SUBLANES, c))

    hc_ref[...] = lax.fori_loop(0, tt // SUBLANES, group, hc_ref[...], unroll=4)

    y = h_ref[...] * xg_ref[...].astype(F32)
    y_ref[...] = (y * _rms_scale(y) * go_ref[...]).astype(BF16)


def _lru_scan(a, b, xg, g_out, tt=1024):
    s, c = a.shape
    blk = pl.BlockSpec((tt, c), lambda i: (i, 0))
    return pl.pallas_call(
        functools.partial(_lru_scan_kernel, tt=tt, c=c),
        grid=(s // tt,),
        in_specs=[blk, blk, blk, pl.BlockSpec((1, c), lambda i: (0, 0))],
        out_specs=blk,
        out_shape=jax.ShapeDtypeStruct((s, c), BF16),
        scratch_shapes=[pltpu.VMEM((tt, c), F32),
                        pltpu.VMEM((SUBLANES, c), F32)],
        compiler_params=_params("arbitrary"),
        name="lru_scan",
    )(a, b, xg, g_out)


def _fox_kernel(r_ref, qt_ref, k_ref, g_ref, vt_ref, *rest, tq, tk, n_cast):
    cast_in, (o_ref, *cast_out) = rest[:n_cast], rest[n_cast:2 * n_cast + 1]
    m_ref, acc_ref, s0_ref, s1_ref, cm_ref = rest[2 * n_cast + 1:]
    for w_in_ref, w_out_ref in zip(cast_in, cast_out):
        w_out_ref[...] = w_in_ref[...].astype(BF16)

    kpq = tq // tk
    assert tq == kpq * tk and kpq % 2 == 0
    hd = pl.program_id(0)
    qi = pl.program_id(1)
    dh = qt_ref.shape[1]
    n_full = kpq * qi
    row = lax.broadcasted_iota(jnp.int32, (LANES, tq), 0)
    ones_q = jnp.where(row < N_BIAS_PIECES, 1.0, 0.0).astype(BF16)
    qt_aug = jnp.concatenate([qt_ref[0], ones_q], axis=0)
    ones_v = jnp.ones((BF16_ROWS, tk), BF16)
    r_q = r_ref[hd, n_full]
    m_ref[...] = jnp.full_like(m_ref, -jnp.inf)
    acc_ref[...] = jnp.zeros_like(acc_ref)
    s_refs = (s0_ref, s1_ref)

    def scores(kb, diag_j, slot):
        c0 = 0 if diag_j is None else diag_j * tk
        k0 = pl.multiple_of(kb * tk, tk)
        k_aug = jnp.concatenate([k_ref[0, pl.ds(k0, tk), :], g_ref[0, pl.ds(k0, tk), :]], axis=1)
        s = jnp.dot(k_aug, qt_aug[:, c0:], preferred_element_type=F32)
        if diag_j is not None:
            kk = lax.broadcasted_iota(jnp.int32, s.shape, 0)
            qq = lax.broadcasted_iota(jnp.int32, s.shape, 1)
            s = jnp.where(kk <= qq, s, -jnp.inf)
        s_refs[slot][:, c0:] = s
        cm_ref[slot, :, c0:] = jnp.max(s, axis=0, keepdims=True)

    def update(kb, diag_j, slot):
        c0 = 0 if diag_j is None else diag_j * tk
        k0 = pl.multiple_of(kb * tk, tk)
        shift = r_ref[hd, kb] - r_q
        m_prev = m_ref[:, c0:]
        m_new = jnp.maximum(m_prev, cm_ref[slot, :, c0:] + shift)
        alpha = jnp.exp2(m_prev - m_new)
        p = jnp.exp2(s_refs[slot][:, c0:] - (m_new - shift)).astype(BF16)
        vt_aug = jnp.concatenate([vt_ref[0, :, pl.ds(k0, tk)], ones_v], axis=0)
        acc_ref[:, c0:] = alpha * acc_ref[:, c0:] + jnp.dot(vt_aug, p, preferred_element_type=F32)
        m_ref[:, c0:] = m_new

    pre = [kpq - 1] + [j for j in range(1, kpq - 1) if j != kpq // 2] + [0]
    post = [kpq // 2] if kpq > 2 else []
    n_pre = len(pre)
    scores(n_full + pre[0], pre[0], 0)
    for p in range(1, n_pre):
        scores(n_full + pre[p], pre[p], p % 2)
        update(n_full + pre[p - 1], pre[p - 1], (p - 1) % 2)

    def body(u, carry):
        for t in range(kpq):
            blk = kpq * u + t
            scores(blk, None, (n_pre + t) % 2)
            prev = jnp.where(u == 0, n_full, blk - 1) if t == 0 else blk - 1
            update(prev, None, (n_pre + t - 1) % 2)
        return carry

    lax.fori_loop(0, qi, body, 0)
    pending = (jnp.where(qi == 0, n_full, n_full - 1), None, (n_pre - 1) % 2)
    for i, j in enumerate(post):
        scores(n_full + j, j, (n_pre + i) % 2)
        update(*pending)
        pending = (n_full + j, j, (n_pre + i) % 2)
    update(*pending)
    out_t = acc_ref[0:dh, :] / acc_ref[dh:dh + 1, :]
    o_ref[...] = jnp.transpose(out_t).astype(o_ref.dtype)


def _fox_attention(r, qt, k, gp, vt, tq, tk, cast_along=()):
    n_heads, s, dh = k.shape
    nq = s // tq
    kern = functools.partial(_fox_kernel, tq=tq, tk=tk, n_cast=len(cast_along))
    head = lambda h, i: (h, 0, 0)
    cast_specs = []
    for wgt in cast_along:
        rows = wgt.shape[0] // (n_heads * nq)
        assert rows * n_heads * nq == wgt.shape[0] and rows % BF16_ROWS == 0
        cast_specs.append(pl.BlockSpec((rows, wgt.shape[1]), lambda h, i: (h * nq + i, 0)))
    return pl.pallas_call(
        kern,
        grid=(n_heads, nq),
        in_specs=[pl.BlockSpec(memory_space=pltpu.SMEM),
                  pl.BlockSpec((1, dh, tq), lambda h, i: (h, 0, i)),
                  pl.BlockSpec((1, s, dh), head),
                  pl.BlockSpec((1, s, LANES), head),
                  pl.BlockSpec((1, dh, s), head)] + cast_specs,
        out_specs=[pl.BlockSpec((tq, dh), lambda h, i: (i, h))] + cast_specs,
        out_shape=[jax.ShapeDtypeStruct((s, n_heads * dh), BF16)]
        + [jax.ShapeDtypeStruct(wgt.shape, BF16) for wgt in cast_along],
        scratch_shapes=[pltpu.VMEM((1, tq), F32),
                        pltpu.VMEM((dh + BF16_ROWS, tq), F32),
                        pltpu.VMEM((tk, tq), F32), pltpu.VMEM((tk, tq), F32),
                        pltpu.VMEM((2, 1, tq), F32)],
        compiler_params=_params("arbitrary", "arbitrary"),
        name="fox_attn",
    )(r, qt, k, gp, vt, *cast_along)


def _out_proj_kernel(yl_ref, ya_ref, x_ref, ga_ref, wl_ref, wa_ref, gp_ref, gt_ref,
                     g2_ref, sh_ref, sc_ref, x1_ref, h2_ref):
    ya = ya_ref[...].astype(F32)
    ya = (ya * _rms_scale(ya) * ga_ref[...]).astype(BF16)
    y = jnp.dot(yl_ref[...], wl_ref[...], preferred_element_type=F32)
    y = y + jnp.dot(ya, wa_ref[...], preferred_element_type=F32)
    x1 = x_ref[...] + (y * _rms_scale(y)) * (gt_ref[...] * gp_ref[...])
    x1_ref[...] = x1
    h2_ref[...] = ((x1 * _rms_scale(x1)) * (g2_ref[...] * (1.0 + sc_ref[...])) + sh_ref[...]).astype(BF16)


def _out_proj(yl, ya, x2d, g_att, w_o, g_post, gt1, g_pre2, sh2, sc2, tm=512):
    s, d = x2d.shape
    c = yl.shape[1]
    assert ya.shape[1] == c and w_o.shape == (2 * c, d)
    row = lambda i: (i, 0)
    vec_d = pl.BlockSpec((1, d), lambda i: (0, 0))
    vec_c = pl.BlockSpec((1, c), lambda i: (0, 0))
    half = lambda blk: pl.BlockSpec((c, d), lambda i: (blk, 0), pipeline_mode=pl.Buffered(1))
    return pl.pallas_call(
        _out_proj_kernel,
        grid=(s // tm,),
        in_specs=[pl.BlockSpec((tm, c), row), pl.BlockSpec((tm, c), row), pl.BlockSpec((tm, d), row),
                  vec_c, half(0), half(1), vec_d, vec_d,
                  vec_d, vec_d, vec_d],
        out_specs=[pl.BlockSpec((tm, d), row), pl.BlockSpec((tm, d), row)],
        out_shape=[jax.ShapeDtypeStruct((s, d), F32), jax.ShapeDtypeStruct((s, d), BF16)],
        compiler_params=_params("arbitrary"),
        name="out_proj",
    )(yl, ya, x2d, g_att, w_o, w_o, g_post, gt1, g_pre2, sh2, sc2)


def _ffn_kernel(h_ref, w1_ref, w2_ref, x_hbm, gp_ref, gt_ref, o_ref, xbuf_ref, xsem):
    i = pl.program_id(0)
    j = pl.program_id(1)
    tm = xbuf_ref.shape[0]

    def residual_copy():
        r0 = pl.multiple_of(i * tm, tm)
        return pltpu.make_async_copy(x_hbm.at[pl.ds(r0, tm), :], xbuf_ref, xsem)

    def partial_sum():
        hid = jnp.dot(h_ref[...], w1_ref[...], preferred_element_type=F32)
        hid = jnp.maximum(hid, 0.0)
        hid = (hid * hid).astype(BF16)
        return jnp.dot(hid, w2_ref[...], preferred_element_type=F32)

    @pl.when(j == 0)
    def _():
        residual_copy().start()
        o_ref[...] = partial_sum()

    @pl.when(j > 0)
    def _():
        o_ref[...] += partial_sum()

    @pl.when(j == pl.num_programs(1) - 1)
    def _():
        residual_copy().wait()
        y = o_ref[...]
        o_ref[...] = xbuf_ref[...] + (y * _rms_scale(y)) * (gt_ref[...] * gp_ref[...])


def _ffn(h2, w1, w2, x1, g_post, gt2, tm=512, tf=2048):
    s, d = x1.shape
    dff = w1.shape[1]
    vec = pl.BlockSpec((1, d), lambda i, j: (0, 0))
    return pl.pallas_call(
        _ffn_kernel,
        grid=(s // tm, dff // tf),
        in_specs=[pl.BlockSpec((tm, d), lambda i, j: (i, 0)),
                  pl.BlockSpec((d, tf), lambda i, j: (0, j)),
                  pl.BlockSpec((tf, d), lambda i, j: (j, 0)),
                  pl.BlockSpec(memory_space=pl.ANY),
                  vec, vec],
        out_specs=pl.BlockSpec((tm, d), lambda i, j: (i, 0)),
        out_shape=jax.ShapeDtypeStruct((s, d), F32),
        scratch_shapes=[pltpu.VMEM((tm, d), F32), pltpu.SemaphoreType.DMA(())],
        compiler_params=_params("arbitrary", "arbitrary"),
        name="ffn",
    )(h2, w1, w2, x1, g_post, gt2)


def _block_diag(w):
    n, bs, _ = w.shape
    eye = jnp.eye(n, dtype=w.dtype)
    return jnp.einsum("nij,nm->nimj", w, eye).reshape(n * bs, n * bs)


def _gate_weights(w_rg, w_ig):
    n, bs, _ = w_rg.shape
    per = MXU_TILE // bs
    grouped = lambda w: jax.vmap(_block_diag)(w.reshape(n // per, per, bs, bs))
    return jnp.concatenate([grouped(w_rg), grouped(w_ig)], axis=-1).astype(BF16)


def kernel(x, c, w_ada, b_ada, g_pre_mix, g_post_mix, g_pre_mlp, g_post_mlp, w_in, conv_w, conv_b,
           w_rg, b_rg, w_ig, b_ig, lru_lambda, b_forget, g_lru_out, g_att_out, w_out, w_ff1, w_ff2):
    bsz, s, d = x.shape
    assert bsz == 1, "kernels are written for a single sequence"
    depth = w_ada.shape[0]
    d_lru = conv_w.shape[-1]
    n_heads = b_forget.shape[-1]
    d_att = n_heads * HEAD_DIM
    assert n_heads <= LANES

    x2d = x.reshape(s, d)
    c_col = c.reshape(d, 1)
    n_main = 2 * d_lru + 3 * d_att
    for l in range(depth):
        mod = _adaln(c_col, w_ada[l], b_ada[l][None, :])
        sh1, sc1, gt1, sh2, sc2, gt2 = jnp.split(mod, 6, axis=-1)

        w, wf = _cast_w_in(jnp.swapaxes(w_in[l], 0, 1), n_main)
        a, b, xg, qt, k, vt, f = _in_proj(
            x2d, g_pre_mix[l][None], sh1, sc1, w, wf, conv_w[l], conv_b[l][None],
            _gate_weights(w_rg[l], w_ig[l]), b_rg[l][None], b_ig[l][None], lru_lambda[l][None], d_att)

        b_pad = jnp.pad(b_forget[l], (0, LANES - n_heads))[None, :]
        gp, gabs = _fgate(f, b_pad, n_heads, ATT_TK)
        r = jnp.transpose(gabs[::ATT_TK, :n_heads])

        y_lru = _lru_scan(a, b, xg, g_lru_out[l][None])
        y_att, wo, w1, w2 = _fox_attention(r, qt, k, gp, vt, ATT_TQ, ATT_TK,
                                           cast_along=(w_out[l], w_ff1[l], w_ff2[l]))
        x2d, h2 = _out_proj(y_lru, y_att, x2d, g_att_out[l][None], wo,
                            g_post_mix[l][None], gt1, g_pre_mlp[l][None], sh2, sc2)
        x2d = _ffn(h2, w1, w2, x2d, g_post_mlp[l][None], gt2)
    return x2d.reshape(bsz, s, d)
```

```python
import functools
import math

import jax
import jax.numpy as jnp
from jax import lax
from jax.experimental import pallas as pl
from jax.experimental.pallas import tpu as pltpu

F32 = jnp.float32
BF16 = jnp.bfloat16

EPS = 1e-6
LRU_C = 8.0
CONV_WIDTH = 4
HEAD_DIM = 128
LOG2E = math.log2(math.e)
LANES = 128
SUBLANES = 8
BF16_ROWS = 16
MXU_TILE = 256
ATT_TK = 512
ATT_TQ = 2048
N_BIAS_PIECES = 3
VMEM_LIMIT_BYTES = 56 * 1024 * 1024


def _params(*sem):
    return pltpu.CompilerParams(dimension_semantics=sem, vmem_limit_bytes=VMEM_LIMIT_BYTES)


def _sigmoid(z):
    return 1.0 / (1.0 + jnp.exp(-z))


def _rms_scale(v):
    return lax.rsqrt(jnp.mean(v * v, axis=-1, keepdims=True) + EPS)


def _resident(shape):
    nd = len(shape)
    return pl.BlockSpec(shape, lambda *_: (0,) * nd, pipeline_mode=pl.Buffered(1))


def _adaln_kernel(c_ref, w_ref, b_ref, o_ref):
    c = c_ref[...]
    ca = c * _sigmoid(c)
    o_ref[...] = jnp.sum(ca * w_ref[...], axis=0, keepdims=True) + b_ref[...]


def _adaln(c_col, w_ada, b_ada, tn=1024):
    d, n = w_ada.shape
    return pl.pallas_call(
        _adaln_kernel,
        grid=(n // tn,),
        in_specs=[pl.BlockSpec((d, 1), lambda j: (0, 0)),
                  pl.BlockSpec((d, tn), lambda j: (0, j)),
                  pl.BlockSpec((1, tn), lambda j: (0, j))],
        out_specs=pl.BlockSpec((1, tn), lambda j: (0, j)),
        out_shape=jax.ShapeDtypeStruct((1, n), F32),
        compiler_params=_params("arbitrary"),
        name="adaln",
    )(c_col, w_ada, b_ada)


def _cast_w_in_kernel(w_ref, tail_ref, wm_ref, wf_ref):
    wm_ref[...] = w_ref[...].astype(BF16)
    n_tail, d = tail_ref.shape
    pad = jnp.zeros((wf_ref.shape[0] - n_tail, d), F32)
    wf_ref[...] = jnp.concatenate([tail_ref[...], pad], axis=0).astype(BF16)


def _cast_w_in(w_t, n_main, tr=512):
    n, d = w_t.shape
    n_tail = n - n_main
    assert n_tail % SUBLANES == 0 and n_tail <= LANES and n_main % n_tail == 0 and n_main % tr == 0
    return pl.pallas_call(
        _cast_w_in_kernel,
        grid=(n_main // tr,),
        in_specs=[pl.BlockSpec((tr, d), lambda i: (i, 0)),
                  pl.BlockSpec((n_tail, d), lambda i: (n_main // n_tail, 0))],
        out_specs=[pl.BlockSpec((tr, d), lambda i: (i, 0)), pl.BlockSpec((LANES, d), lambda i: (0, 0))],
        out_shape=[jax.ShapeDtypeStruct((n_main, d), BF16), jax.ShapeDtypeStruct((LANES, d), BF16)],
        compiler_params=_params("arbitrary"),
        name="cast_w_in",
    )(w_t, w_t)


def _dot_t(a, b_t):
    return lax.dot_general(a, b_t, (((1,), (1,)), ((), ())), preferred_element_type=F32)


def _in_proj_kernel(x_ref, g_ref, sh_ref, sc_ref, wu_ref, wq_ref, wk_ref, wv_ref, wf_ref,
                    cw_ref, cb_ref, wg_ref, brg_ref, big_ref, lam_ref,
                    a_ref, b_ref, xg_ref, qt_ref, k_ref, vt_ref, f_ref, ext_ref, *, n_heads, q_scale):
    tm = x_ref.shape[0]
    c = a_ref.shape[1]
    pad = SUBLANES

    @pl.when(pl.program_id(0) == 0)
    def _():
        ext_ref[0:pad, :] = jnp.zeros((pad, c), F32)

    x = x_ref[...]
    h = ((x * _rms_scale(x)) * (g_ref[...] * (1.0 + sc_ref[...])) + sh_ref[...]).astype(BF16)
    u = _dot_t(h, wu_ref[...])
    ug = u[:, c:]
    gelu = 0.5 * ug * (1.0 + jnp.tanh(math.sqrt(2.0 / math.pi) * (ug + 0.044715 * (ug * ug * ug))))
    xg_ref[...] = gelu.astype(BF16)

    ul = u[:, :c]
    ext_ref[pad:pad + tm, :] = ul
    xc = ul * cw_ref[CONV_WIDTH - 1:CONV_WIDTH, :] + cb_ref[...]
    for kk in range(CONV_WIDTH - 1):
        back = CONV_WIDTH - 1 - kk
        xc = xc + ext_ref[pad - back:pad - back + tm, :] * cw_ref[kk:kk + 1, :]
    ext_ref[0:pad, :] = ul[tm - pad:tm, :]

    gw = wg_ref.shape[1]
    gr, gi = [], []
    for grp in range(c // gw):
        gg = jnp.dot(xc[:, grp * gw:(grp + 1) * gw].astype(BF16), wg_ref[grp], preferred_element_type=F32)
        gr.append(gg[:, :gw])
        gi.append(gg[:, gw:])
    r = 0.5 + 0.5 * jnp.tanh(0.5 * (jnp.concatenate(gr, axis=1) + brg_ref[...]))
    ig = 0.5 + 0.5 * jnp.tanh(0.5 * (jnp.concatenate(gi, axis=1) + big_ref[...]))
    lam = lam_ref[...]
    sp = jnp.maximum(-lam, 0.0) + jnp.log(1.0 + jnp.exp(-jnp.abs(lam)))
    a = jnp.exp((-LRU_C) * r * sp)
    t = 1.0 - a * a
    mult = jnp.where(t > 0.0, t * lax.rsqrt(t), 0.0)
    a_ref[...] = a
    b_ref[...] = mult * (ig * xc)

    q = _dot_t(h, wq_ref[...]) * q_scale
    k = _dot_t(h, wk_ref[...])
    v = _dot_t(h, wv_ref[...])
    for hd in range(n_heads):
        sl = slice(hd * HEAD_DIM, (hd + 1) * HEAD_DIM)
        qt_ref[hd] = jnp.transpose(q[:, sl]).astype(BF16)
        k_ref[hd] = k[:, sl].astype(BF16)
        vt_ref[hd] = jnp.transpose(v[:, sl]).astype(BF16)
    f_ref[...] = _dot_t(h, wf_ref[...])


def _in_proj(x2d, g, sh, sc, w, wf, conv_w, conv_b, w_gates, b_rg, b_ig, lam, d_att, tm=512):
    s, d = x2d.shape
    c = conv_w.shape[1]
    n_u = 2 * c
    n_heads = d_att // HEAD_DIM
    assert n_u % d_att == 0
    q_blk = n_u // d_att

    def cols(width, blk):
        return pl.BlockSpec((width, d), lambda i: (blk, 0), pipeline_mode=pl.Buffered(1))

    row = lambda i: (i, 0)
    vec = pl.BlockSpec((1, d), lambda i: (0, 0))
    kern = functools.partial(_in_proj_kernel, n_heads=n_heads, q_scale=HEAD_DIM ** -0.5 * LOG2E)
    k_shape = jax.ShapeDtypeStruct((n_heads, s, HEAD_DIM), BF16)
    k_spec = pl.BlockSpec((n_heads, tm, HEAD_DIM), lambda i: (0, i, 0))
    t_shape = jax.ShapeDtypeStruct((n_heads, HEAD_DIM, s), BF16)
    t_spec = pl.BlockSpec((n_heads, HEAD_DIM, tm), lambda i: (0, 0, i))
    vec_c = pl.BlockSpec((1, c), lambda i: (0, 0))
    lru_spec = pl.BlockSpec((tm, c), row)
    return pl.pallas_call(
        kern,
        grid=(s // tm,),
        in_specs=[pl.BlockSpec((tm, d), row), vec, vec, vec,
                  cols(n_u, 0), cols(d_att, q_blk), cols(d_att, q_blk + 1), cols(d_att, q_blk + 2),
                  _resident(wf.shape),
                  pl.BlockSpec((CONV_WIDTH, c), lambda i: (0, 0)), vec_c, _resident(w_gates.shape),
                  vec_c, vec_c, vec_c],
        out_specs=[lru_spec, lru_spec, lru_spec, t_spec, k_spec, t_spec,
                   pl.BlockSpec((tm, LANES), row)],
        out_shape=[jax.ShapeDtypeStruct((s, c), F32), jax.ShapeDtypeStruct((s, c), F32),
                   jax.ShapeDtypeStruct((s, c), BF16), t_shape, k_shape, t_shape,
                   jax.ShapeDtypeStruct((s, LANES), F32)],
        scratch_shapes=[pltpu.VMEM((tm + SUBLANES, c), F32)],
        compiler_params=_params("arbitrary"),
        name="in_proj",
    )(x2d, g, sh, sc, w, w, w, w, wf, conv_w, conv_b, w_gates, b_rg, b_ig, lam)


def _fgate_kernel(f_ref, b_ref, sel_ref, gp_ref, gabs_ref, carry_ref, *, tm, tk, n_heads):
    @pl.when(pl.program_id(0) == 0)
    def _():
        carry_ref[...] = jnp.zeros_like(carry_ref)

    z = f_ref[...] + b_ref[...]
    inc = (jnp.log(1.0 + jnp.exp(-jnp.abs(z))) - jnp.minimum(z, 0.0)) * LOG2E
    row = lax.broadcasted_iota(jnp.int32, (tm, LANES), 0)
    acc = inc
    d = 1
    while d < tm:
        acc = acc + jnp.where(row >= d, pltpu.roll(acc, d, axis=0), 0.0)
        d *= 2
    carry = carry_ref[0:1, :]
    gabs_ref[...] = acc + carry
    carry_ref[...] = jnp.broadcast_to(acc[tm - 1:tm, :] + carry, carry_ref.shape)

    for blk in range(tm // tk):
        a = acc[blk * tk:(blk + 1) * tk, :]
        rem = a - a[0:1, :]
        pieces = []
        for _ in range(N_BIAS_PIECES):
            pc = rem.astype(BF16)
            pieces.append(pc)
            rem = rem - pc.astype(F32)
        spread = jnp.dot(jnp.concatenate(pieces, axis=1), sel_ref[...], preferred_element_type=F32)
        for hd in range(n_heads):
            gp_ref[hd, blk * tk:(blk + 1) * tk, :] = spread[:, hd * LANES:(hd + 1) * LANES].astype(BF16)


def _fgate(f, b_pad, n_heads, tk, tm=1024):
    s = f.shape[0]
    src = jnp.arange(N_BIAS_PIECES * LANES)[:, None]
    dst = jnp.arange(n_heads * LANES)[None, :]
    sel = ((src // LANES == dst % LANES) & (src % LANES == dst // LANES)).astype(BF16)
    return pl.pallas_call(
        functools.partial(_fgate_kernel, tm=tm, tk=tk, n_heads=n_heads),
        grid=(s // tm,),
        in_specs=[pl.BlockSpec((tm, LANES), lambda i: (i, 0)),
                  pl.BlockSpec((1, LANES), lambda i: (0, 0)),
                  _resident(sel.shape)],
        out_specs=[pl.BlockSpec((n_heads, tm, LANES), lambda i: (0, i, 0)),
                   pl.BlockSpec((tm, LANES), lambda i: (i, 0))],
        out_shape=[jax.ShapeDtypeStruct((n_heads, s, LANES), BF16),
                   jax.ShapeDtypeStruct((s, LANES), F32)],
        scratch_shapes=[pltpu.VMEM((SUBLANES, LANES), F32)],
        compiler_params=_params("arbitrary"),
        name="fgate",
    )(f, b_pad, sel)


def _lru_scan_kernel(a_ref, b_ref, xg_ref, go_ref, y_ref, h_ref, hc_ref, *, tt, c):
    @pl.when(pl.program_id(0) == 0)
    def _():
        hc_ref[...] = jnp.zeros_like(hc_ref)

    row = lax.broadcasted_iota(jnp.int32, (SUBLANES, c), 0)

    def group(gi, h_prev):
        r0 = pl.multiple_of(gi * SUBLANES, SUBLANES)
        av = a_ref[pl.ds(r0, SUBLANES), :]
        bv = b_ref[pl.ds(r0, SUBLANES), :]
        d = 1
        while d < SUBLANES:
            a_sh = jnp.where(row >= d, pltpu.roll(av, d, axis=0), 1.0)
            b_sh = jnp.where(row >= d, pltpu.roll(bv, d, axis=0), 0.0)
            bv = bv + av * b_sh
            av = av * a_sh
            d *= 2
        hv = bv + av * h_prev
        h_ref[pl.ds(r0, SUBLANES), :] = hv
        return jnp.broadcast_to(hv[SUBLANES - 1:SUBLANES, :], (SUBLANES, c))

    hc_ref[...] = lax.fori_loop(0, tt // SUBLANES, group, hc_ref[...], unroll=4)

    y = h_ref[...] * xg_ref[...].astype(F32)
    y_ref[...] = (y * _rms_scale(y) * go_ref[...]).astype(BF16)


def _lru_scan(a, b, xg, g_out, tt=1024):
    s, c = a.shape
    blk = pl.BlockSpec((tt, c), lambda i: (i, 0))
    return pl.pallas_call(
        functools.partial(_lru_scan_kernel, tt=tt, c=c),
        grid=(s // tt,),
        in_specs=[blk, blk, blk, pl.BlockSpec((1, c), lambda i: (0, 0))],
        out_specs=blk,
        out_shape=jax.ShapeDtypeStruct((s, c), BF16),
        scratch_shapes=[pltpu.VMEM((tt, c), F32),
                        pltpu.VMEM((SUBLANES, c), F32)],
        compiler_params=_params("arbitrary"),
        name="lru_scan",
    )(a, b, xg, g_out)


def _fox_kernel(r_ref, qt_ref, k_ref, g_ref, vt_ref, *rest, tq, tk, n_cast):
    cast_in, (o_ref, *cast_out) = rest[:n_cast], rest[n_cast:2 * n_cast + 1]
    m_ref, acc_ref, s0_ref, s1_ref, cm_ref = rest[2 * n_cast + 1:]
    for w_in_ref, w_out_ref in zip(cast_in, cast_out):
        w_out_ref[...] = w_in_ref[...].astype(BF16)

    kpq = tq // tk
    assert tq == kpq * tk and kpq % 2 == 0
    hd = pl.program_id(0)
    qi = pl.program_id(1)
    dh = qt_ref.shape[1]
    n_full = kpq * qi
    row = lax.broadcasted_iota(jnp.int32, (LANES, tq), 0)
    ones_q = jnp.where(row < N_BIAS_PIECES, 1.0, 0.0).astype(BF16)
    qt_aug = jnp.concatenate([qt_ref[0], ones_q], axis=0)
    ones_v = jnp.ones((BF16_ROWS, tk), BF16)
    r_q = r_ref[hd, n_full]
    m_ref[...] = jnp.full_like(m_ref, -jnp.inf)
    acc_ref[...] = jnp.zeros_like(acc_ref)
    s_refs = (s0_ref, s1_ref)

    def scores(kb, diag_j, slot):
        c0 = 0 if diag_j is None else diag_j * tk
        k0 = pl.multiple_of(kb * tk, tk)
        k_aug = jnp.concatenate([k_ref[0, pl.ds(k0, tk), :], g_ref[0, pl.ds(k0, tk), :]], axis=1)
        s = jnp.dot(k_aug, qt_aug[:, c0:], preferred_element_type=F32)
        if diag_j is not None:
            kk = lax.broadcasted_iota(jnp.int32, s.shape, 0)
            qq = lax.broadcasted_iota(jnp.int32, s.shape, 1)
            s = jnp.where(kk <= qq, s, -jnp.inf)
        s_refs[slot][:, c0:tq] = s
        cm_ref[slot, :, c0:] = jnp.max(s, axis=0, keepdims=True)

    def update(kb, diag_j, slot):
        c0 = 0 if diag_j is None else diag_j * tk
        k0 = pl.multiple_of(kb * tk, tk)
        shift = r_ref[hd, kb] - r_q
        m_prev = m_ref[:, c0:]
        m_new = jnp.maximum(m_prev, cm_ref[slot, :, c0:] + shift)
        alpha = jnp.exp2(m_prev - m_new)
        p = jnp.exp2(s_refs[slot][:, c0:tq] - (m_new - shift)).astype(BF16)
        vt_aug = jnp.concatenate([vt_ref[0, :, pl.ds(k0, tk)], ones_v], axis=0)
        acc_ref[:, c0:tq] = alpha * acc_ref[:, c0:tq] + jnp.dot(vt_aug, p, preferred_element_type=F32)
        m_ref[:, c0:] = m_new

    pre = [kpq - 1] + [j for j in range(1, kpq - 1) if j != kpq // 2] + [0]
    post = [kpq // 2] if kpq > 2 else []
    n_pre = len(pre)
    scores(n_full + pre[0], pre[0], 0)
    for p in range(1, n_pre):
        scores(n_full + pre[p], pre[p], p % 2)
        update(n_full + pre[p - 1], pre[p - 1], (p - 1) % 2)

    def body(u, carry):
        for t in range(kpq):
            blk = kpq * u + t
            scores(blk, None, (n_pre + t) % 2)
            prev = jnp.where(u == 0, n_full, blk - 1) if t == 0 else blk - 1
            update(prev, None, (n_pre + t - 1) % 2)
        return carry

    lax.fori_loop(0, qi, body, 0)
    pending = (jnp.where(qi == 0, n_full, n_full - 1), None, (n_pre - 1) % 2)
    for i, j in enumerate(post):
        scores(n_full + j, j, (n_pre + i) % 2)
        update(*pending)
        pending = (n_full + j, j, (n_pre + i) % 2)
    update(*pending)
    out_t = acc_ref[0:dh, 0:tq] / acc_ref[dh:dh + 1, 0:tq]
    o_ref[...] = jnp.transpose(out_t).astype(o_ref.dtype)


def _fox_attention(r, qt, k, gp, vt, tq, tk, cast_along=()):
    n_heads, s, dh = k.shape
    nq = s // tq
    kern = functools.partial(_fox_kernel, tq=tq, tk=tk, n_cast=len(cast_along))
    head = lambda h, i: (h, 0, 0)
    cast_specs = []
    for wgt in cast_along:
        rows = wgt.shape[0] // (n_heads * nq)
        assert rows * n_heads * nq == wgt.shape[0] and rows % BF16_ROWS == 0
        cast_specs.append(pl.BlockSpec((rows, wgt.shape[1]), lambda h, i: (h * nq + i, 0)))
    return pl.pallas_call(
        kern,
        grid=(n_heads, nq),
        in_specs=[pl.BlockSpec(memory_space=pltpu.SMEM),
                  pl.BlockSpec((1, dh, tq), lambda h, i: (h, 0, i)),
                  pl.BlockSpec((1, s, dh), head),
                  pl.BlockSpec((1, s, LANES), head),
                  pl.BlockSpec((1, dh, s), head)] + cast_specs,
        out_specs=[pl.BlockSpec((tq, dh), lambda h, i: (i, h))] + cast_specs,
        out_shape=[jax.ShapeDtypeStruct((s, n_heads * dh), BF16)]
        + [jax.ShapeDtypeStruct(wgt.shape, BF16) for wgt in cast_along],
        scratch_shapes=[pltpu.VMEM((1, tq), F32),
                        pltpu.VMEM((dh + BF16_ROWS, tq + LANES), F32),
                        pltpu.VMEM((tk, tq + LANES), F32), pltpu.VMEM((tk, tq + LANES), F32),
                        pltpu.VMEM((2, 1, tq), F32)],
        compiler_params=_params("arbitrary", "arbitrary"),
        name="fox_attn",
    )(r, qt, k, gp, vt, *cast_along)


def _out_proj_kernel(yl_ref, ya_ref, x_ref, ga_ref, wl_ref, wa_ref, gp_ref, gt_ref,
                     g2_ref, sh_ref, sc_ref, x1_ref, h2_ref):
    ya = ya_ref[...].astype(F32)
    ya = (ya * _rms_scale(ya) * ga_ref[...]).astype(BF16)
    y = jnp.dot(yl_ref[...], wl_ref[...], preferred_element_type=F32)
    y = y + jnp.dot(ya, wa_ref[...], preferred_element_type=F32)
    x1 = x_ref[...] + (y * _rms_scale(y)) * (gt_ref[...] * gp_ref[...])
    x1_ref[...] = x1
    h2_ref[...] = ((x1 * _rms_scale(x1)) * (g2_ref[...] * (1.0 + sc_ref[...])) + sh_ref[...]).astype(BF16)


def _out_proj(yl, ya, x2d, g_att, w_o, g_post, gt1, g_pre2, sh2, sc2, tm=512):
    s, d = x2d.shape
    c = yl.shape[1]
    assert ya.shape[1] == c and w_o.shape == (2 * c, d)
    row = lambda i: (i, 0)
    vec_d = pl.BlockSpec((1, d), lambda i: (0, 0))
    vec_c = pl.BlockSpec((1, c), lambda i: (0, 0))
    half = lambda blk: pl.BlockSpec((c, d), lambda i: (blk, 0), pipeline_mode=pl.Buffered(1))
    return pl.pallas_call(
        _out_proj_kernel,
        grid=(s // tm,),
        in_specs=[pl.BlockSpec((tm, c), row), pl.BlockSpec((tm, c), row), pl.BlockSpec((tm, d), row),
                  vec_c, half(0), half(1), vec_d, vec_d,
                  vec_d, vec_d, vec_d],
        out_specs=[pl.BlockSpec((tm, d), row), pl.BlockSpec((tm, d), row)],
        out_shape=[jax.ShapeDtypeStruct((s, d), F32), jax.ShapeDtypeStruct((s, d), BF16)],
        compiler_params=_params("arbitrary"),
        name="out_proj",
    )(yl, ya, x2d, g_att, w_o, w_o, g_post, gt1, g_pre2, sh2, sc2)


def _ffn_kernel(h_ref, w1_ref, w2_ref, x_hbm, gp_ref, gt_ref, o_ref, xbuf_ref, xsem):
    i = pl.program_id(0)
    j = pl.program_id(1)
    tm = xbuf_ref.shape[0]

    def residual_copy():
        r0 = pl.multiple_of(i * tm, tm)
        return pltpu.make_async_copy(x_hbm.at[pl.ds(r0, tm), :], xbuf_ref, xsem)

    def partial_sum():
        hid = jnp.dot(h_ref[...], w1_ref[...], preferred_element_type=F32)
        hid = jnp.maximum(hid, 0.0)
        hid = (hid * hid).astype(BF16)
        return jnp.dot(hid, w2_ref[...], preferred_element_type=F32)

    @pl.when(j == 0)
    def _():
        residual_copy().start()
        o_ref[...] = partial_sum()

    @pl.when(j > 0)
    def _():
        o_ref[...] += partial_sum()

    @pl.when(j == pl.num_programs(1) - 1)
    def _():
        residual_copy().wait()
        y = o_ref[...]
        o_ref[...] = xbuf_ref[...] + (y * _rms_scale(y)) * (gt_ref[...] * gp_ref[...])


def _ffn(h2, w1, w2, x1, g_post, gt2, tm=512, tf=2048):
    s, d = x1.shape
    dff = w1.shape[1]
    vec = pl.BlockSpec((1, d), lambda i, j: (0, 0))
    return pl.pallas_call(
        _ffn_kernel,
        grid=(s // tm, dff // tf),
        in_specs=[pl.BlockSpec((tm, d), lambda i, j: (i, 0)),
                  pl.BlockSpec((d, tf), lambda i, j: (0, j)),
                  pl.BlockSpec((tf, d), lambda i, j: (j, 0)),
                  pl.BlockSpec(memory_space=pl.ANY),
                  vec, vec],
        out_specs=pl.BlockSpec((tm, d), lambda i, j: (i, 0)),
        out_shape=jax.ShapeDtypeStruct((s, d), F32),
        scratch_shapes=[pltpu.VMEM((tm, d), F32), pltpu.SemaphoreType.DMA(())],
        compiler_params=_params("arbitrary", "arbitrary"),
        name="ffn",
    )(h2, w1, w2, x1, g_post, gt2)


def _block_diag(w):
    n, bs, _ = w.shape
    eye = jnp.eye(n, dtype=w.dtype)
    return jnp.einsum("nij,nm->nimj", w, eye).reshape(n * bs, n * bs)


def _gate_weights(w_rg, w_ig):
    n, bs, _ = w_rg.shape
    per = MXU_TILE // bs
    grouped = lambda w: jax.vmap(_block_diag)(w.reshape(n // per, per, bs, bs))
    return jnp.concatenate([grouped(w_rg), grouped(w_ig)], axis=-1).astype(BF16)


def kernel(x, c, w_ada, b_ada, g_pre_mix, g_post_mix, g_pre_mlp, g_post_mlp, w_in, conv_w, conv_b,
           w_rg, b_rg, w_ig, b_ig, lru_lambda, b_forget, g_lru_out, g_att_out, w_out, w_ff1, w_ff2):
    bsz, s, d = x.shape
    assert bsz == 1, "kernels are written for a single sequence"
    depth = w_ada.shape[0]
    d_lru = conv_w.shape[-1]
    n_heads = b_forget.shape[-1]
    d_att = n_heads * HEAD_DIM
    assert n_heads <= LANES

    x2d = x.reshape(s, d)
    c_col = c.reshape(d, 1)
    n_main = 2 * d_lru + 3 * d_att
    for l in range(depth):
        mod = _adaln(c_col, w_ada[l], b_ada[l][None, :])
        sh1, sc1, gt1, sh2, sc2, gt2 = jnp.split(mod, 6, axis=-1)

        w, wf = _cast_w_in(jnp.swapaxes(w_in[l], 0, 1), n_main)
        a, b, xg, qt, k, vt, f = _in_proj(
            x2d, g_pre_mix[l][None], sh1, sc1, w, wf, conv_w[l], conv_b[l][None],
            _gate_weights(w_rg[l], w_ig[l]), b_rg[l][None], b_ig[l][None], lru_lambda[l][None], d_att)

        b_pad = jnp.pad(b_forget[l], (0, LANES - n_heads))[None, :]
        gp, gabs = _fgate(f, b_pad, n_heads, ATT_TK)
        r = jnp.transpose(gabs[::ATT_TK, :n_heads])

        y_lru = _lru_scan(a, b, xg, g_lru_out[l][None])
        y_att, wo, w1, w2 = _fox_attention(r, qt, k, gp, vt, ATT_TQ, ATT_TK,
                                           cast_along=(w_out[l], w_ff1[l], w_ff2[l]))
        x2d, h2 = _out_proj(y_lru, y_att, x2d, g_att_out[l][None], wo,
                            g_post_mix[l][None], gt1, g_pre_mlp[l][None], sh2, sc2)
        x2d = _ffn(h2, w1, w2, x2d, g_post_mlp[l][None], gt2)
    return x2d.reshape(bsz, s, d)
```

```python
import functools
import math

import jax
import jax.numpy as jnp
from jax import lax
from jax.experimental import pallas as pl
from jax.experimental.pallas import tpu as pltpu

F32 = jnp.float32
BF16 = jnp.bfloat16

EPS = 1e-6
LRU_C = 8.0
CONV_WIDTH = 4
HEAD_DIM = 128
LOG2E = math.log2(math.e)
LANES = 128
SUBLANES = 8
BF16_ROWS = 16
MXU_TILE = 256
ATT_TK = 512
ATT_TQ = 4096
N_BIAS_PIECES = 3
VMEM_LIMIT_BYTES = 56 * 1024 * 1024
ATT_VMEM_LIMIT_BYTES = 62 * 1024 * 1024


def _params(*sem):
    return pltpu.CompilerParams(dimension_semantics=sem, vmem_limit_bytes=VMEM_LIMIT_BYTES)


def _sigmoid(z):
    return 1.0 / (1.0 + jnp.exp(-z))


def _rms_scale(v):
    return lax.rsqrt(jnp.mean(v * v, axis=-1, keepdims=True) + EPS)


def _resident(shape):
    nd = len(shape)
    return pl.BlockSpec(shape, lambda *_: (0,) * nd, pipeline_mode=pl.Buffered(1))


def _adaln_kernel(c_ref, w_ref, b_ref, o_ref):
    c = c_ref[...]
    ca = c * _sigmoid(c)
    o_ref[...] = jnp.sum(ca * w_ref[...], axis=0, keepdims=True) + b_ref[...]


def _adaln(c_col, w_ada, b_ada, tn=1024):
    d, n = w_ada.shape
    return pl.pallas_call(
        _adaln_kernel,
        grid=(n // tn,),
        in_specs=[pl.BlockSpec((d, 1), lambda j: (0, 0)),
                  pl.BlockSpec((d, tn), lambda j: (0, j)),
                  pl.BlockSpec((1, tn), lambda j: (0, j))],
        out_specs=pl.BlockSpec((1, tn), lambda j: (0, j)),
        out_shape=jax.ShapeDtypeStruct((1, n), F32),
        compiler_params=_params("arbitrary"),
        name="adaln",
    )(c_col, w_ada, b_ada)


def _cast_w_in_kernel(w_ref, tail_ref, wm_ref, wf_ref):
    wm_ref[...] = w_ref[...].astype(BF16)
    n_tail, d = tail_ref.shape
    pad = jnp.zeros((wf_ref.shape[0] - n_tail, d), F32)
    wf_ref[...] = jnp.concatenate([tail_ref[...], pad], axis=0).astype(BF16)


def _cast_w_in(w_t, n_main, tr=512):
    n, d = w_t.shape
    n_tail = n - n_main
    assert n_tail % SUBLANES == 0 and n_tail <= LANES and n_main % n_tail == 0 and n_main % tr == 0
    return pl.pallas_call(
        _cast_w_in_kernel,
        grid=(n_main // tr,),
        in_specs=[pl.BlockSpec((tr, d), lambda i: (i, 0)),
                  pl.BlockSpec((n_tail, d), lambda i: (n_main // n_tail, 0))],
        out_specs=[pl.BlockSpec((tr, d), lambda i: (i, 0)), pl.BlockSpec((LANES, d), lambda i: (0, 0))],
        out_shape=[jax.ShapeDtypeStruct((n_main, d), BF16), jax.ShapeDtypeStruct((LANES, d), BF16)],
        compiler_params=_params("arbitrary"),
        name="cast_w_in",
    )(w_t, w_t)


def _dot_t(a, b_t):
    return lax.dot_general(a, b_t, (((1,), (1,)), ((), ())), preferred_element_type=F32)


def _in_proj_kernel(x_ref, g_ref, sh_ref, sc_ref, wu_ref, wq_ref, wk_ref, wv_ref, wf_ref,
                    cw_ref, cb_ref, wg_ref, brg_ref, big_ref, lam_ref,
                    a_ref, b_ref, xg_ref, qt_ref, k_ref, vt_ref, f_ref, ext_ref, *, n_heads, q_scale):
    tm = x_ref.shape[0]
    c = a_ref.shape[1]
    pad = SUBLANES

    @pl.when(pl.program_id(0) == 0)
    def _():
        ext_ref[0:pad, :] = jnp.zeros((pad, c), F32)

    x = x_ref[...]
    h = ((x * _rms_scale(x)) * (g_ref[...] * (1.0 + sc_ref[...])) + sh_ref[...]).astype(BF16)
    u = _dot_t(h, wu_ref[...])
    ug = u[:, c:]
    gelu = 0.5 * ug * (1.0 + jnp.tanh(math.sqrt(2.0 / math.pi) * (ug + 0.044715 * (ug * ug * ug))))
    xg_ref[...] = gelu.astype(BF16)

    ul = u[:, :c]
    ext_ref[pad:pad + tm, :] = ul
    xc = ul * cw_ref[CONV_WIDTH - 1:CONV_WIDTH, :] + cb_ref[...]
    for kk in range(CONV_WIDTH - 1):
        back = CONV_WIDTH - 1 - kk
        xc = xc + ext_ref[pad - back:pad - back + tm, :] * cw_ref[kk:kk + 1, :]
    ext_ref[0:pad, :] = ul[tm - pad:tm, :]

    gw = wg_ref.shape[1]
    gr, gi = [], []
    for grp in range(c // gw):
        gg = jnp.dot(xc[:, grp * gw:(grp + 1) * gw].astype(BF16), wg_ref[grp], preferred_element_type=F32)
        gr.append(gg[:, :gw])
        gi.append(gg[:, gw:])
    r = 0.5 + 0.5 * jnp.tanh(0.5 * (jnp.concatenate(gr, axis=1) + brg_ref[...]))
    ig = 0.5 + 0.5 * jnp.tanh(0.5 * (jnp.concatenate(gi, axis=1) + big_ref[...]))
    lam = lam_ref[...]
    sp = jnp.maximum(-lam, 0.0) + jnp.log(1.0 + jnp.exp(-jnp.abs(lam)))
    a = jnp.exp((-LRU_C) * r * sp)
    t = 1.0 - a * a
    mult = jnp.where(t > 0.0, t * lax.rsqrt(t), 0.0)
    a_ref[...] = a
    b_ref[...] = mult * (ig * xc)

    q = _dot_t(h, wq_ref[...]) * q_scale
    k = _dot_t(h, wk_ref[...])
    v = _dot_t(h, wv_ref[...])
    for hd in range(n_heads):
        sl = slice(hd * HEAD_DIM, (hd + 1) * HEAD_DIM)
        qt_ref[hd] = jnp.transpose(q[:, sl]).astype(BF16)
        k_ref[hd] = k[:, sl].astype(BF16)
        vt_ref[hd] = jnp.transpose(v[:, sl]).astype(BF16)
    f_ref[...] = _dot_t(h, wf_ref[...])


def _in_proj(x2d, g, sh, sc, w, wf, conv_w, conv_b, w_gates, b_rg, b_ig, lam, d_att, tm=512):
    s, d = x2d.shape
    c = conv_w.shape[1]
    n_u = 2 * c
    n_heads = d_att // HEAD_DIM
    assert n_u % d_att == 0
    q_blk = n_u // d_att

    def cols(width, blk):
        return pl.BlockSpec((width, d), lambda i: (blk, 0), pipeline_mode=pl.Buffered(1))

    row = lambda i: (i, 0)
    vec = pl.BlockSpec((1, d), lambda i: (0, 0))
    kern = functools.partial(_in_proj_kernel, n_heads=n_heads, q_scale=HEAD_DIM ** -0.5 * LOG2E)
    k_shape = jax.ShapeDtypeStruct((n_heads, s, HEAD_DIM), BF16)
    k_spec = pl.BlockSpec((n_heads, tm, HEAD_DIM), lambda i: (0, i, 0))
    t_shape = jax.ShapeDtypeStruct((n_heads, HEAD_DIM, s), BF16)
    t_spec = pl.BlockSpec((n_heads, HEAD_DIM, tm), lambda i: (0, 0, i))
    vec_c = pl.BlockSpec((1, c), lambda i: (0, 0))
    lru_spec = pl.BlockSpec((tm, c), row)
    return pl.pallas_call(
        kern,
        grid=(s // tm,),
        in_specs=[pl.BlockSpec((tm, d), row), vec, vec, vec,
                  cols(n_u, 0), cols(d_att, q_blk), cols(d_att, q_blk + 1), cols(d_att, q_blk + 2),
                  _resident(wf.shape),
                  pl.BlockSpec((CONV_WIDTH, c), lambda i: (0, 0)), vec_c, _resident(w_gates.shape),
                  vec_c, vec_c, vec_c],
        out_specs=[lru_spec, lru_spec, lru_spec, t_spec, k_spec, t_spec,
                   pl.BlockSpec((tm, LANES), row)],
        out_shape=[jax.ShapeDtypeStruct((s, c), F32), jax.ShapeDtypeStruct((s, c), F32),
                   jax.ShapeDtypeStruct((s, c), BF16), t_shape, k_shape, t_shape,
                   jax.ShapeDtypeStruct((s, LANES), F32)],
        scratch_shapes=[pltpu.VMEM((tm + SUBLANES, c), F32)],
        compiler_params=_params("arbitrary"),
        name="in_proj",
    )(x2d, g, sh, sc, w, w, w, w, wf, conv_w, conv_b, w_gates, b_rg, b_ig, lam)


def _fgate_kernel(f_ref, b_ref, sel_ref, gp_ref, gabs_ref, carry_ref, *, tm, tk, n_heads):
    @pl.when(pl.program_id(0) == 0)
    def _():
        carry_ref[...] = jnp.zeros_like(carry_ref)

    z = f_ref[...] + b_ref[...]
    inc = (jnp.log(1.0 + jnp.exp(-jnp.abs(z))) - jnp.minimum(z, 0.0)) * LOG2E
    row = lax.broadcasted_iota(jnp.int32, (tm, LANES), 0)
    acc = inc
    d = 1
    while d < tm:
        acc = acc + jnp.where(row >= d, pltpu.roll(acc, d, axis=0), 0.0)
        d *= 2
    carry = carry_ref[0:1, :]
    gabs_ref[...] = acc + carry
    carry_ref[...] = jnp.broadcast_to(acc[tm - 1:tm, :] + carry, carry_ref.shape)

    for blk in range(tm // tk):
        a = acc[blk * tk:(blk + 1) * tk, :]
        rem = a - a[0:1, :]
        pieces = []
        for _ in range(N_BIAS_PIECES):
            pc = rem.astype(BF16)
            pieces.append(pc)
            rem = rem - pc.astype(F32)
        spread = jnp.dot(jnp.concatenate(pieces, axis=1), sel_ref[...], preferred_element_type=F32)
        for hd in range(n_heads):
            gp_ref[hd, blk * tk:(blk + 1) * tk, :] = spread[:, hd * LANES:(hd + 1) * LANES].astype(BF16)


def _fgate(f, b_pad, n_heads, tk, tm=1024):
    s = f.shape[0]
    src = jnp.arange(N_BIAS_PIECES * LANES)[:, None]
    dst = jnp.arange(n_heads * LANES)[None, :]
    sel = ((src // LANES == dst % LANES) & (src % LANES == dst // LANES)).astype(BF16)
    return pl.pallas_call(
        functools.partial(_fgate_kernel, tm=tm, tk=tk, n_heads=n_heads),
        grid=(s // tm,),
        in_specs=[pl.BlockSpec((tm, LANES), lambda i: (i, 0)),
                  pl.BlockSpec((1, LANES), lambda i: (0, 0)),
                  _resident(sel.shape)],
        out_specs=[pl.BlockSpec((n_heads, tm, LANES), lambda i: (0, i, 0)),
                   pl.BlockSpec((tm, LANES), lambda i: (i, 0))],
        out_shape=[jax.ShapeDtypeStruct((n_heads, s, LANES), BF16),
                   jax.ShapeDtypeStruct((s, LANES), F32)],
        scratch_shapes=[pltpu.VMEM((SUBLANES, LANES), F32)],
        compiler_params=_params("arbitrary"),
        name="fgate",
    )(f, b_pad, sel)


def _lru_scan_kernel(a_ref, b_ref, xg_ref, go_ref, y_ref, h_ref, hc_ref, *, tt, c):
    @pl.when(pl.program_id(0) == 0)
    def _():
        hc_ref[...] = jnp.zeros_like(hc_ref)

    row = lax.broadcasted_iota(jnp.int32, (SUBLANES, c), 0)

    def group(gi, h_prev):
        r0 = pl.multiple_of(gi * SUBLANES, SUBLANES)
        av = a_ref[pl.ds(r0, SUBLANES), :]
        bv = b_ref[pl.ds(r0, SUBLANES), :]
        d = 1
        while d < SUBLANES:
            a_sh = jnp.where(row >= d, pltpu.roll(av, d, axis=0), 1.0)
            b_sh = jnp.where(row >= d, pltpu.roll(bv, d, axis=0), 0.0)
            bv = bv + av * b_sh
            av = av * a_sh
            d *= 2
        hv = bv + av * h_prev
        h_ref[pl.ds(r0, SUBLANES), :] = hv
        return jnp.broadcast_to(hv[SUBLANES - 1:SUBLANES, :], (SUBLANES, c))

    hc_ref[...] = lax.fori_loop(0, tt // SUBLANES, group, hc_ref[...], unroll=4)

    y = h_ref[...] * xg_ref[...].astype(F32)
    y_ref[...] = (y * _rms_scale(y) * go_ref[...]).astype(BF16)


def _lru_scan(a, b, xg, g_out, tt=1024):
    s, c = a.shape
    blk = pl.BlockSpec((tt, c), lambda i: (i, 0))
    return pl.pallas_call(
        functools.partial(_lru_scan_kernel, tt=tt, c=c),
        grid=(s // tt,),
        in_specs=[blk, blk, blk, pl.BlockSpec((1, c), lambda i: (0, 0))],
        out_specs=blk,
        out_shape=jax.ShapeDtypeStruct((s, c), BF16),
        scratch_shapes=[pltpu.VMEM((tt, c), F32),
                        pltpu.VMEM((SUBLANES, c), F32)],
        compiler_params=_params("arbitrary"),
        name="lru_scan",
    )(a, b, xg, g_out)


def _fox_kernel(r_ref, qt_ref, k_ref, g_ref, vt_ref, *rest, tq, tk, n_cast):
    cast_in, (o_ref, *cast_out) = rest[:n_cast], rest[n_cast:2 * n_cast + 1]
    m_ref, acc_ref, s0_ref, s1_ref, cm_ref = rest[2 * n_cast + 1:]
    for w_in_ref, w_out_ref in zip(cast_in, cast_out):
        w_out_ref[...] = w_in_ref[...].astype(BF16)

    kpq = tq // tk
    assert tq == kpq * tk and kpq % 2 == 0
    hd = pl.program_id(0)
    qi = pl.program_id(1)
    dh = qt_ref.shape[1]
    n_full = kpq * qi
    row = lax.broadcasted_iota(jnp.int32, (LANES, tq), 0)
    ones_q = jnp.where(row < N_BIAS_PIECES, 1.0, 0.0).astype(BF16)
    qt_aug = jnp.concatenate([qt_ref[0], ones_q], axis=0)
    ones_v = jnp.ones((BF16_ROWS, tk), BF16)
    r_q = r_ref[hd, n_full]
    m_ref[...] = jnp.full_like(m_ref, -jnp.inf)
    acc_ref[...] = jnp.zeros_like(acc_ref)
    s_refs = (s0_ref, s1_ref)

    def scores(kb, diag_j, slot):
        c0 = 0 if diag_j is None else diag_j * tk
        k0 = pl.multiple_of(kb * tk, tk)
        k_aug = jnp.concatenate([k_ref[0, pl.ds(k0, tk), :], g_ref[0, pl.ds(k0, tk), :]], axis=1)
        s = jnp.dot(k_aug, qt_aug[:, c0:], preferred_element_type=F32)
        if diag_j is not None:
            kk = lax.broadcasted_iota(jnp.int32, s.shape, 0)
            qq = lax.broadcasted_iota(jnp.int32, s.shape, 1)
            s = jnp.where(kk <= qq, s, -jnp.inf)
        s_refs[slot][:, c0:] = s
        cm_ref[slot, :, c0:] = jnp.max(s, axis=0, keepdims=True)

    def update(kb, diag_j, slot):
        c0 = 0 if diag_j is None else diag_j * tk
        k0 = pl.multiple_of(kb * tk, tk)
        shift = r_ref[hd, kb] - r_q
        m_prev = m_ref[:, c0:]
        m_new = jnp.maximum(m_prev, cm_ref[slot, :, c0:] + shift)
        alpha = jnp.exp2(m_prev - m_new)
        p = jnp.exp2(s_refs[slot][:, c0:] - (m_new - shift)).astype(BF16)
        vt_aug = jnp.concatenate([vt_ref[0, :, pl.ds(k0, tk)], ones_v], axis=0)
        acc_ref[:, c0:] = alpha * acc_ref[:, c0:] + jnp.dot(vt_aug, p, preferred_element_type=F32)
        m_ref[:, c0:] = m_new

    pre = [kpq - 1] + [j for j in range(1, kpq - 1) if j != kpq // 2] + [0]
    post = [kpq // 2] if kpq > 2 else []
    n_pre = len(pre)
    scores(n_full + pre[0], pre[0], 0)
    for p in range(1, n_pre):
        scores(n_full + pre[p], pre[p], p % 2)
        update(n_full + pre[p - 1], pre[p - 1], (p - 1) % 2)

    def body(u, carry):
        for t in range(kpq):
            blk = kpq * u + t
            scores(blk, None, (n_pre + t) % 2)
            prev = jnp.where(u == 0, n_full, blk - 1) if t == 0 else blk - 1
            update(prev, None, (n_pre + t - 1) % 2)
        return carry

    lax.fori_loop(0, qi, body, 0)
    pending = (jnp.where(qi == 0, n_full, n_full - 1), None, (n_pre - 1) % 2)
    for i, j in enumerate(post):
        scores(n_full + j, j, (n_pre + i) % 2)
        update(*pending)
        pending = (n_full + j, j, (n_pre + i) % 2)
    update(*pending)
    out_t = acc_ref[0:dh, :] / acc_ref[dh:dh + 1, :]
    o_ref[...] = jnp.transpose(out_t).astype(o_ref.dtype)


def _fox_attention(r, qt, k, gp, vt, tq, tk, cast_along=()):
    n_heads, s, dh = k.shape
    nq = s // tq
    kern = functools.partial(_fox_kernel, tq=tq, tk=tk, n_cast=len(cast_along))
    head = lambda h, i: (h, 0, 0)
    cast_specs = []
    for wgt in cast_along:
        rows = wgt.shape[0] // (n_heads * nq)
        assert rows * n_heads * nq == wgt.shape[0] and rows % BF16_ROWS == 0
        cast_specs.append(pl.BlockSpec((rows, wgt.shape[1]), lambda h, i: (h * nq + i, 0)))
    return pl.pallas_call(
        kern,
        grid=(n_heads, nq),
        in_specs=[pl.BlockSpec(memory_space=pltpu.SMEM),
                  pl.BlockSpec((1, dh, tq), lambda h, i: (h, 0, i)),
                  pl.BlockSpec((1, s, dh), head, pipeline_mode=pl.Buffered(1)),
                  pl.BlockSpec((1, s, LANES), head, pipeline_mode=pl.Buffered(1)),
                  pl.BlockSpec((1, dh, s), head, pipeline_mode=pl.Buffered(1))] + cast_specs,
        out_specs=[pl.BlockSpec((tq, dh), lambda h, i: (i, h))] + cast_specs,
        out_shape=[jax.ShapeDtypeStruct((s, n_heads * dh), BF16)]
        + [jax.ShapeDtypeStruct(wgt.shape, BF16) for wgt in cast_along],
        scratch_shapes=[pltpu.VMEM((1, tq), F32),
                        pltpu.VMEM((dh + BF16_ROWS, tq), F32),
                        pltpu.VMEM((tk, tq), F32), pltpu.VMEM((tk, tq), F32),
                        pltpu.VMEM((2, 1, tq), F32)],
        compiler_params=pltpu.CompilerParams(dimension_semantics=("arbitrary", "arbitrary"),
                                             vmem_limit_bytes=ATT_VMEM_LIMIT_BYTES),
        name="fox_attn",
    )(r, qt, k, gp, vt, *cast_along)


def _out_proj_kernel(yl_ref, ya_ref, x_ref, ga_ref, wl_ref, wa_ref, gp_ref, gt_ref,
                     g2_ref, sh_ref, sc_ref, x1_ref, h2_ref):
    ya = ya_ref[...].astype(F32)
    ya = (ya * _rms_scale(ya) * ga_ref[...]).astype(BF16)
    y = jnp.dot(yl_ref[...], wl_ref[...], preferred_element_type=F32)
    y = y + jnp.dot(ya, wa_ref[...], preferred_element_type=F32)
    x1 = x_ref[...] + (y * _rms_scale(y)) * (gt_ref[...] * gp_ref[...])
    x1_ref[...] = x1
    h2_ref[...] = ((x1 * _rms_scale(x1)) * (g2_ref[...] * (1.0 + sc_ref[...])) + sh_ref[...]).astype(BF16)


def _out_proj(yl, ya, x2d, g_att, w_o, g_post, gt1, g_pre2, sh2, sc2, tm=512):
    s, d = x2d.shape
    c = yl.shape[1]
    assert ya.shape[1] == c and w_o.shape == (2 * c, d)
    row = lambda i: (i, 0)
    vec_d = pl.BlockSpec((1, d), lambda i: (0, 0))
    vec_c = pl.BlockSpec((1, c), lambda i: (0, 0))
    half = lambda blk: pl.BlockSpec((c, d), lambda i: (blk, 0), pipeline_mode=pl.Buffered(1))
    return pl.pallas_call(
        _out_proj_kernel,
        grid=(s // tm,),
        in_specs=[pl.BlockSpec((tm, c), row), pl.BlockSpec((tm, c), row), pl.BlockSpec((tm, d), row),
                  vec_c, half(0), half(1), vec_d, vec_d,
                  vec_d, vec_d, vec_d],
        out_specs=[pl.BlockSpec((tm, d), row), pl.BlockSpec((tm, d), row)],
        out_shape=[jax.ShapeDtypeStruct((s, d), F32), jax.ShapeDtypeStruct((s, d), BF16)],
        compiler_params=_params("arbitrary"),
        name="out_proj",
    )(yl, ya, x2d, g_att, w_o, w_o, g_post, gt1, g_pre2, sh2, sc2)


def _ffn_kernel(h_ref, w1_ref, w2_ref, x_hbm, gp_ref, gt_ref, o_ref, xbuf_ref, xsem):
    i = pl.program_id(0)
    j = pl.program_id(1)
    tm = xbuf_ref.shape[0]

    def residual_copy():
        r0 = pl.multiple_of(i * tm, tm)
        return pltpu.make_async_copy(x_hbm.at[pl.ds(r0, tm), :], xbuf_ref, xsem)

    def partial_sum():
        hid = jnp.dot(h_ref[...], w1_ref[...], preferred_element_type=F32)
        hid = jnp.maximum(hid, 0.0)
        hid = (hid * hid).astype(BF16)
        return jnp.dot(hid, w2_ref[...], preferred_element_type=F32)

    @pl.when(j == 0)
    def _():
        residual_copy().start()
        o_ref[...] = partial_sum()

    @pl.when(j > 0)
    def _():
        o_ref[...] += partial_sum()

    @pl.when(j == pl.num_programs(1) - 1)
    def _():
        residual_copy().wait()
        y = o_ref[...]
        o_ref[...] = xbuf_ref[...] + (y * _rms_scale(y)) * (gt_ref[...] * gp_ref[...])


def _ffn(h2, w1, w2, x1, g_post, gt2, tm=512, tf=2048):
    s, d = x1.shape
    dff = w1.shape[1]
    vec = pl.BlockSpec((1, d), lambda i, j: (0, 0))
    return pl.pallas_call(
        _ffn_kernel,
        grid=(s // tm, dff // tf),
        in_specs=[pl.BlockSpec((tm, d), lambda i, j: (i, 0)),
                  pl.BlockSpec((d, tf), lambda i, j: (0, j)),
                  pl.BlockSpec((tf, d), lambda i, j: (j, 0)),
                  pl.BlockSpec(memory_space=pl.ANY),
                  vec, vec],
        out_specs=pl.BlockSpec((tm, d), lambda i, j: (i, 0)),
        out_shape=jax.ShapeDtypeStruct((s, d), F32),
        scratch_shapes=[pltpu.VMEM((tm, d), F32), pltpu.SemaphoreType.DMA(())],
        compiler_params=_params("arbitrary", "arbitrary"),
        name="ffn",
    )(h2, w1, w2, x1, g_post, gt2)


def _block_diag(w):
    n, bs, _ = w.shape
    eye = jnp.eye(n, dtype=w.dtype)
    return jnp.einsum("nij,nm->nimj", w, eye).reshape(n * bs, n * bs)


def _gate_weights(w_rg, w_ig):
    n, bs, _ = w_rg.shape
    per = MXU_TILE // bs
    grouped = lambda w: jax.vmap(_block_diag)(w.reshape(n // per, per, bs, bs))
    return jnp.concatenate([grouped(w_rg), grouped(w_ig)], axis=-1).astype(BF16)


def kernel(x, c, w_ada, b_ada, g_pre_mix, g_post_mix, g_pre_mlp, g_post_mlp, w_in, conv_w, conv_b,
           w_rg, b_rg, w_ig, b_ig, lru_lambda, b_forget, g_lru_out, g_att_out, w_out, w_ff1, w_ff2):
    bsz, s, d = x.shape
    assert bsz == 1, "kernels are written for a single sequence"
    depth = w_ada.shape[0]
    d_lru = conv_w.shape[-1]
    n_heads = b_forget.shape[-1]
    d_att = n_heads * HEAD_DIM
    assert n_heads <= LANES

    x2d = x.reshape(s, d)
    c_col = c.reshape(d, 1)
    n_main = 2 * d_lru + 3 * d_att
    for l in range(depth):
        mod = _adaln(c_col, w_ada[l], b_ada[l][None, :])
        sh1, sc1, gt1, sh2, sc2, gt2 = jnp.split(mod, 6, axis=-1)

        w, wf = _cast_w_in(jnp.swapaxes(w_in[l], 0, 1), n_main)
        a, b, xg, qt, k, vt, f = _in_proj(
            x2d, g_pre_mix[l][None], sh1, sc1, w, wf, conv_w[l], conv_b[l][None],
            _gate_weights(w_rg[l], w_ig[l]), b_rg[l][None], b_ig[l][None], lru_lambda[l][None], d_att)

        b_pad = jnp.pad(b_forget[l], (0, LANES - n_heads))[None, :]
        gp, gabs = _fgate(f, b_pad, n_heads, ATT_TK)
        r = jnp.transpose(gabs[::ATT_TK, :n_heads])

        y_lru = _lru_scan(a, b, xg, g_lru_out[l][None])
        y_att, wo, w1, w2 = _fox_attention(r, qt, k, gp, vt, ATT_TQ, ATT_TK,
                                           cast_along=(w_out[l], w_ff1[l], w_ff2[l]))
        x2d, h2 = _out_proj(y_lru, y_att, x2d, g_att_out[l][None], wo,
                            g_post_mix[l][None], gt1, g_pre_mlp[l][None], sh2, sc2)
        x2d = _ffn(h2, w1, w2, x2d, g_post_mlp[l][None], gt2)
    return x2d.reshape(bsz, s, d)
```
